```python
import math
import jax, jax.numpy as jnp
from jax import lax
import numpy as np

D_MODEL = 2048
BATCH = 4
SEQ = 4096
DEPTH = 1

MLA_NOPE = 128
MLA_ROPE = 64
MLA_V = 128
MLA_HEADS = D_MODEL // MLA_V
MLA_Q_RANK = 768
MLA_KV_RANK = 512
MLA_QK = MLA_NOPE + MLA_ROPE
ROPE_THETA = 10000.0
SWA_HEAD_DIM = 64
SWA_HEADS = D_MODEL // SWA_HEAD_DIM
SWA_KV_HEADS = 4
SWA_GROUP = SWA_HEADS // SWA_KV_HEADS
WINDOW = 128
BLOCK = 128
REL_BUCKETS = 32
REL_MAX_DIST = 128
D_FF = 5632
CONV_WIDTH = 3
EPS = 1e-6
NEG = -1e30

MLA_IN = MLA_Q_RANK + MLA_KV_RANK + MLA_ROPE
SWA_Q = SWA_HEADS * SWA_HEAD_DIM
SWA_KV = SWA_KV_HEADS * SWA_HEAD_DIM
N_BRANCH = 2
IN_COLS = MLA_IN + SWA_Q + 2 * SWA_KV + N_BRANCH * D_MODEL

kernel_name = "hybrid_mla_swa_convffn_block"


def rms_norm(x, g):
    xf = x.astype(jnp.float32)
    y = xf * lax.rsqrt(jnp.mean(xf * xf, axis=-1, keepdims=True) + EPS)
    return (y * g.astype(jnp.float32)).astype(x.dtype)


def rope_tables(seq):
    pos = jnp.arange(seq, dtype=jnp.float32)
    inv = ROPE_THETA ** (-jnp.arange(0, MLA_ROPE, 2, dtype=jnp.float32) / MLA_ROPE)
    ang = pos[:, None] * inv[None, :]
    ang = jnp.concatenate([ang, ang], axis=-1)
    return jnp.cos(ang), jnp.sin(ang)


def apply_rope(x, cos, sin):
    half = x.shape[-1] // 2
    x1, x2 = x[..., :half], x[..., half:]
    rot = jnp.concatenate([-x2, x1], axis=-1)
    return x * cos.astype(x.dtype) + rot * sin.astype(x.dtype)


def t5_bucket(dist):
    max_exact = REL_BUCKETS // 2
    n = jnp.maximum(dist, 0)
    large = max_exact + (jnp.log(jnp.maximum(n, 1).astype(jnp.float32) / max_exact)
                         / math.log(REL_MAX_DIST / max_exact)
                         * (REL_BUCKETS - max_exact)).astype(jnp.int32)
    large = jnp.minimum(large, REL_BUCKETS - 1)
    return jnp.where(n < max_exact, n, large)


def mla_branch(cq, ckv, k_rope, g_q, w_uq, g_kv, w_ukv):
    B, S, _ = cq.shape
    nb = S // BLOCK
    cos, sin = rope_tables(S)
    q = (rms_norm(cq, g_q) @ w_uq).reshape(B, S, MLA_HEADS, MLA_QK)
    q_nope = q[..., :MLA_NOPE]
    q_rope = apply_rope(q[..., MLA_NOPE:], cos[:, None, :], sin[:, None, :])
    kv = (rms_norm(ckv, g_kv) @ w_ukv).reshape(B, S, MLA_HEADS, MLA_NOPE + MLA_V)
    k_nope, v = kv[..., :MLA_NOPE], kv[..., MLA_NOPE:]
    k_rope = apply_rope(k_rope, cos, sin)
    scale = MLA_QK ** -0.5
    qn_blocks = q_nope.reshape(B, nb, BLOCK, MLA_HEADS, MLA_NOPE).transpose(1, 0, 2, 3, 4)
    qr_blocks = q_rope.reshape(B, nb, BLOCK, MLA_HEADS, MLA_ROPE).transpose(1, 0, 2, 3, 4)
    kpos = jnp.arange(S)

    def one_block(args):
        qn, qr, i = args
        s = (jnp.einsum('bqhd,bkhd->bhqk', qn, k_nope)
             + jnp.einsum('bqhd,bkd->bhqk', qr, k_rope)).astype(jnp.float32) * scale
        qpos = i * BLOCK + jnp.arange(BLOCK)
        s = jnp.where(kpos[None, :] <= qpos[:, None], s, NEG)
        p = jax.nn.softmax(s, axis=-1).astype(v.dtype)
        return jnp.einsum('bhqk,bkhd->bqhd', p, v)

    out = lax.map(one_block, (qn_blocks, qr_blocks, jnp.arange(nb)))
    return out.transpose(1, 0, 2, 3, 4).reshape(B, S, MLA_HEADS * MLA_V)


def swa_branch(q, k, v, rel_bias, sinks):
    B, S, _ = q.shape
    nb = S // BLOCK
    q = q.reshape(B, nb, BLOCK, SWA_KV_HEADS, SWA_GROUP, SWA_HEAD_DIM)

    def band(t):
        t = t.reshape(B, S, SWA_KV_HEADS, SWA_HEAD_DIM)
        t = jnp.pad(t, ((0, 0), (BLOCK, 0), (0, 0), (0, 0)))
        t = t.reshape(B, nb + 1, BLOCK, SWA_KV_HEADS, SWA_HEAD_DIM)
        return jnp.concatenate([t[:, :-1], t[:, 1:]], axis=2)

    kb, vb = band(k), band(v)
    s = jnp.einsum('bnqhgd,bnshd->bhgnqs', q, kb).astype(jnp.float32) * (SWA_HEAD_DIM ** -0.5)
    a = jnp.arange(BLOCK)
    bidx = jnp.arange(2 * BLOCK)
    dist = BLOCK + a[:, None] - bidx[None, :]
    bias = rel_bias[t5_bucket(dist)].astype(jnp.float32)
    bias = bias.transpose(2, 0, 1).reshape(SWA_KV_HEADS, SWA_GROUP, 1, BLOCK, 2 * BLOCK)
    kpos = jnp.arange(nb)[:, None, None] * BLOCK - BLOCK + bidx[None, None, :]
    mask = (dist >= 0)[None] & (dist < WINDOW)[None] & (kpos >= 0)
    s = jnp.where(mask, s + bias, NEG)
    sink = sinks.astype(jnp.float32).reshape(SWA_KV_HEADS, SWA_GROUP, 1, 1, 1)
    m = jnp.maximum(jnp.max(s, axis=-1, keepdims=True), sink)
    e = jnp.exp(s - m)
    p = e / (jnp.sum(e, axis=-1, keepdims=True) + jnp.exp(sink - m))
    o = jnp.einsum('bhgnqs,bnshd->bnqhgd', p.astype(vb.dtype), vb)
    return o.reshape(B, S, SWA_Q)


def causal_dwconv(u, w, b):
    S = u.shape[1]
    up = jnp.pad(u, ((0, 0), (CONV_WIDTH - 1, 0), (0, 0)))
    y = b
    for j in range(CONV_WIDTH):
        y = y + w[j] * up[:, j:j + S]
    return y


def setup_inputs(seed: int = 0) -> dict:
    key = jax.random.key(seed)
    ks = jax.random.split(key, 24)
    f32 = jnp.float32
    D, L = D_MODEL, DEPTH

    def nrm(k, shape, scale):
        return jax.random.normal(k, shape, f32) * scale

    def gain(k, shape):
        return 1.0 + 0.1 * jax.random.normal(k, shape, f32)

    return {
        "x": nrm(ks[0], (BATCH, SEQ, D), 1.0),
        "c": nrm(ks[1], (BATCH, D), 1.0),
        "w_ada": nrm(ks[2], (L, D, 6 * D), 0.5 * D ** -0.5),
        "b_ada": nrm(ks[3], (L, 6 * D), 0.02),
        "g_pre_mix": gain(ks[4], (L, D)),
        "g_post_mix": gain(ks[5], (L, D)),
        "w_in": nrm(ks[6], (L, D, IN_COLS), D ** -0.5),
        "g_q_lat": gain(ks[7], (L, MLA_Q_RANK)),
        "w_uq": nrm(ks[8], (L, MLA_Q_RANK, MLA_HEADS * MLA_QK), MLA_Q_RANK ** -0.5),
        "g_kv_lat": gain(ks[9], (L, MLA_KV_RANK)),
        "w_ukv": nrm(ks[10], (L, MLA_KV_RANK, MLA_HEADS * (MLA_NOPE + MLA_V)), MLA_KV_RANK ** -0.5),
        "rel_bias": nrm(ks[11], (REL_BUCKETS, SWA_HEADS), 0.5),
        "sinks": nrm(ks[12], (L, SWA_HEADS), 1.0),
        "w_o": nrm(ks[13], (L, D, D), D ** -0.5),
        "g_pre_ffn": gain(ks[14], (L, D)),
        "g_post_ffn": gain(ks[15], (L, D)),
        "w_up": nrm(ks[16], (L, D, 2 * D_FF), D ** -0.5),
        "conv_w": nrm(ks[17], (L, CONV_WIDTH, 2 * D_FF), CONV_WIDTH ** -0.5),
        "conv_b": nrm(ks[18], (L, 2 * D_FF), 0.02),
        "w_down": nrm(ks[19], (L, D_FF, D), D_FF ** -0.5),
    }


def reference(x, c, w_ada, b_ada, g_pre_mix, g_post_mix, w_in, g_q_lat, w_uq, g_kv_lat,
              w_ukv, rel_bias, sinks, w_o, g_pre_ffn, g_post_ffn, w_up, conv_w, conv_b, w_down):
    D = D_MODEL
    c_act = jax.nn.silu(c)
    for l in range(DEPTH):
        mod = (c_act @ w_ada[l] + b_ada[l])[:, None, :]
        sh1, sc1, gt1, sh2, sc2, gt2 = jnp.split(mod, 6, axis=-1)

        h = rms_norm(x, g_pre_mix[l]) * (1.0 + sc1) + sh1
        z = h @ w_in[l]
        o0 = 0
        cq = z[..., o0:o0 + MLA_Q_RANK]; o0 += MLA_Q_RANK
        ckv = z[..., o0:o0 + MLA_KV_RANK]; o0 += MLA_KV_RANK
        kr = z[..., o0:o0 + MLA_ROPE]; o0 += MLA_ROPE
        qs = z[..., o0:o0 + SWA_Q]; o0 += SWA_Q
        ks_ = z[..., o0:o0 + SWA_KV]; o0 += SWA_KV
        vs = z[..., o0:o0 + SWA_KV]; o0 += SWA_KV
        gates = jax.nn.sigmoid(z[..., o0:o0 + N_BRANCH * D])
        g_a, g_b = gates[..., :D], gates[..., D:]

        o_a = mla_branch(cq, ckv, kr, g_q_lat[l], w_uq[l], g_kv_lat[l], w_ukv[l])
        o_b = swa_branch(qs, ks_, vs, rel_bias, sinks[l])
        mix = (g_a * o_a + g_b * o_b) @ w_o[l]
        x = x + gt1 * rms_norm(mix, g_post_mix[l])

        h = rms_norm(x, g_pre_ffn[l]) * (1.0 + sc2) + sh2
        u = causal_dwconv(h @ w_up[l], conv_w[l], conv_b[l])
        y = (jax.nn.silu(u[..., :D_FF]) * u[..., D_FF:]) @ w_down[l]
        x = x + gt2 * rms_norm(y, g_post_ffn[l])
    return x
```

```python
import functools
import math

import jax
import jax.numpy as jnp
from jax import lax
from jax.experimental import pallas as pl
from jax.experimental.pallas import tpu as pltpu

F32 = jnp.float32
BF16 = jnp.bfloat16

D_MODEL = 2048
MLA_NOPE = 128
MLA_ROPE = 64
MLA_V = 128
MLA_HEADS = D_MODEL // MLA_V
MLA_Q_RANK = 768
MLA_KV_RANK = 512
MLA_QK = MLA_NOPE + MLA_ROPE
ROPE_THETA = 10000.0
SWA_HEAD_DIM = 64
SWA_HEADS = D_MODEL // SWA_HEAD_DIM
SWA_KV_HEADS = 4
SWA_GROUP = SWA_HEADS // SWA_KV_HEADS
SWA_KV = SWA_KV_HEADS * SWA_HEAD_DIM
WINDOW = 128
BLOCK = 128
REL_BUCKETS = 32
REL_MAX_DIST = 128
D_FF = 5632
CONV_WIDTH = 3
EPS = 1e-6
NEG = -1e30
LOG2E = 1.4426950408889634

VMEM_LIMIT_BYTES = 56 * 1024 * 1024
LANES = 128
BF16_ROWS = 16

Z_QS = 0
Z_GA = 2048
Z_GB = 4096
Z_CQ = 6144
Z_KS = 6912
Z_CKV = 7168
Z_VS = 7680
Z_KR = 7936
Z_KRR = 8064
Z_COLS = 8192

SH1, SC1, GT1, SH2, SC2, GT2 = range(6)


def _cparams(sem):
    return pltpu.CompilerParams(dimension_semantics=sem, vmem_limit_bytes=VMEM_LIMIT_BYTES)


def _rms(x, g):
    ms = jnp.mean(x * x, axis=-1, keepdims=True)
    return x * lax.rsqrt(ms + EPS) * g


def _ada_kernel(c_ref, w_ref, b_ref, o_ref):
    c = c_ref[...]
    ca = (c * jax.nn.sigmoid(c)).astype(BF16)
    o_ref[...] = jnp.dot(ca, w_ref[...].astype(BF16), preferred_element_type=F32) + b_ref[...]


def _ada(c8, w_ada, b_ada):
    n = w_ada.shape[1]
    tn = 1024
    return pl.pallas_call(
        _ada_kernel,
        grid=(n // tn,),
        in_specs=[
            pl.BlockSpec((8, D_MODEL), lambda j: (0, 0)),
            pl.BlockSpec((D_MODEL, tn), lambda j: (0, j)),
            pl.BlockSpec((1, tn), lambda j: (0, j)),
        ],
        out_specs=pl.BlockSpec((8, tn), lambda j: (0, j)),
        out_shape=jax.ShapeDtypeStruct((8, n), F32),
        compiler_params=_cparams(("arbitrary",)),
        name="ada",
    )(c8, w_ada, b_ada)


def _fill_normed(x_ref, g, sc, sh, h_ref, row0, nrows, chunk):
    def body(i, carry):
        r = pl.multiple_of(i * chunk, chunk)
        y = _rms(x_ref[pl.ds(r, chunk), :], g)
        h_ref[pl.ds(row0 + r, chunk), :] = (y * (1.0 + sc) + sh).astype(BF16)
        return carry

    lax.fori_loop(0, nrows // chunk, body, 0)


def _inproj_kernel(x_ref, g_ref, mod_ref, w_ref, o_ref, h_ref, *, tm):
    @pl.when(pl.program_id(1) == 0)
    def _():
        _fill_normed(x_ref, g_ref[...], mod_ref[SC1:SC1 + 1, :], mod_ref[SH1:SH1 + 1, :],
                     h_ref, 0, tm, 128)

    o_ref[...] = jnp.dot(h_ref[...], w_ref[...], preferred_element_type=F32).astype(o_ref.dtype)


def _inproj(x2, g, mod, w, seq):
    t, d = x2.shape
    n = w.shape[1]
    tm, tn = 1024, 1024
    tpb = seq // tm
    return pl.pallas_call(
        functools.partial(_inproj_kernel, tm=tm),
        grid=(t // tm, n // tn),
        in_specs=[
            pl.BlockSpec((tm, d), lambda i, j: (i, 0)),
            pl.BlockSpec((1, d), lambda i, j: (0, 0)),
            pl.BlockSpec((None, 6, d), lambda i, j: (i // tpb, 0, 0)),
            pl.BlockSpec((d, tn), lambda i, j: (0, j)),
        ],
        out_specs=pl.BlockSpec((tm, tn), lambda i, j: (i, j)),
        out_shape=jax.ShapeDtypeStruct((t, n), BF16),
        scratch_shapes=[pltpu.VMEM((tm, d), BF16)],
        compiler_params=_cparams(("parallel", "arbitrary")),
        name="inproj",
    )(x2, g, mod, w)


def _rope_chunks(a, b, cos, sin):
    n = a.shape[1] // LANES
    out = [a[:, LANES * i:LANES * (i + 1)] * cos + b[:, LANES * i:LANES * (i + 1)] * sin
           for i in range(n)]
    return out[0] if n == 1 else jnp.concatenate(out, axis=1)


def _qproj_kernel(cq_ref, g_ref, wn_ref, wr_ref, wrr_ref, cos_ref, sin_ref, qn_ref, qr_ref):
    cqn = _rms(cq_ref[...].astype(F32), g_ref[...]).astype(BF16)
    qn_ref[...] = jnp.dot(cqn, wn_ref[...], preferred_element_type=F32).astype(BF16)
    a = jnp.dot(cqn, wr_ref[...], preferred_element_type=F32)
    b = jnp.dot(cqn, wrr_ref[...], preferred_element_type=F32)
    qr_ref[...] = _rope_chunks(a, b, cos_ref[...], sin_ref[...]).astype(BF16)


def _qproj(z, g, wn, wr, wrr, cos2, sin2, seq):
    t = z.shape[0]
    tm = 512
    spb = seq // tm
    hn = MLA_HEADS * MLA_NOPE
    hr = MLA_HEADS * MLA_ROPE
    return pl.pallas_call(
        _qproj_kernel,
        grid=(t // tm,),
        in_specs=[
            pl.BlockSpec((tm, MLA_Q_RANK), lambda i: (i, Z_CQ // MLA_Q_RANK)),
            pl.BlockSpec((1, MLA_Q_RANK), lambda i: (0, 0)),
            pl.BlockSpec((MLA_Q_RANK, hn), lambda i: (0, 0)),
            pl.BlockSpec((MLA_Q_RANK, hr), lambda i: (0, 0)),
            pl.BlockSpec((MLA_Q_RANK, hr), lambda i: (0, 0)),
            pl.BlockSpec((tm, LANES), lambda i: (i % spb, 0)),
            pl.BlockSpec((tm, LANES), lambda i: (i % spb, 0)),
        ],
        out_specs=[
            pl.BlockSpec((tm, hn), lambda i: (i, 0)),
            pl.BlockSpec((tm, hr), lambda i: (i, 0)),
        ],
        out_shape=[jax.ShapeDtypeStruct((t, hn), BF16), jax.ShapeDtypeStruct((t, hr), BF16)],
        compiler_params=_cparams(("parallel",)),
        name="qproj",
    )(z, g, wn, wr, wrr, cos2, sin2)


def _kvproj_kernel(ckv_ref, g_ref, wk_ref, wv_ref, kr_ref, krr_ref, cos_ref, sin_ref,
                   kn_ref, v_ref, kro_ref):
    ckvn = _rms(ckv_ref[...].astype(F32), g_ref[...]).astype(BF16)
    kn_ref[...] = jnp.dot(ckvn, wk_ref[...], preferred_element_type=F32).astype(BF16)
    v_ref[...] = jnp.dot(ckvn, wv_ref[...], preferred_element_type=F32).astype(BF16)
    kro_ref[...] = (kr_ref[...].astype(F32) * cos_ref[...]
                    + krr_ref[...].astype(F32) * sin_ref[...]).astype(BF16)


def _kvproj(z, g, wk, wv, cos2, sin2, seq):
    t = z.shape[0]
    tm = 512
    spb = seq // tm
    hn = MLA_HEADS * MLA_NOPE
    hv = MLA_HEADS * MLA_V
    return pl.pallas_call(
        _kvproj_kernel,
        grid=(t // tm,),
        in_specs=[
            pl.BlockSpec((tm, MLA_KV_RANK), lambda i: (i, Z_CKV // MLA_KV_RANK)),
            pl.BlockSpec((1, MLA_KV_RANK), lambda i: (0, 0)),
            pl.BlockSpec((MLA_KV_RANK, hn), lambda i: (0, 0)),
            pl.BlockSpec((MLA_KV_RANK, hv), lambda i: (0, 0)),
            pl.BlockSpec((tm, LANES), lambda i: (i, Z_KR // LANES)),
            pl.BlockSpec((tm, LANES), lambda i: (i, Z_KRR // LANES)),
            pl.BlockSpec((tm, LANES), lambda i: (i % spb, 0)),
            pl.BlockSpec((tm, LANES), lambda i: (i % spb, 0)),
        ],
        out_specs=[
            pl.BlockSpec((tm, hn), lambda i: (i, 0)),
            pl.BlockSpec((tm, hv), lambda i: (i, 0)),
            pl.BlockSpec((tm, LANES), lambda i: (i, 0)),
        ],
        out_shape=[jax.ShapeDtypeStruct((t, hn), BF16), jax.ShapeDtypeStruct((t, hv), BF16),
                   jax.ShapeDtypeStruct((t, LANES), BF16)],
        compiler_params=_cparams(("parallel",)),
        name="kvproj",
    )(z, g, wk, wv, z, z, cos2, sin2)


def _mla_kernel(qn_ref, qr_ref, kn_ref, kr_ref, v_ref, o_ref, *, tq):
    head = pl.program_id(1)
    qi = pl.program_id(2)
    qr = qr_ref[...]
    lane = lax.broadcasted_iota(jnp.int32, qr.shape, 1)
    lo = (head % 2) * MLA_ROPE
    keep = (lane >= lo) & (lane < lo + MLA_ROPE)
    q = jnp.concatenate([qn_ref[...], jnp.where(keep, qr, jnp.zeros_like(qr))], axis=1)

    def step(k0, carry, diagonal):
        m, l, acc = carry
        kc = jnp.concatenate([kn_ref[pl.ds(k0, tq), :], kr_ref[pl.ds(k0, tq), :]], axis=1)
        s = lax.dot_general(q, kc, (((1,), (1,)), ((), ())), preferred_element_type=F32)
        if diagonal:
            row = lax.broadcasted_iota(jnp.int32, s.shape, 0)
            col = lax.broadcasted_iota(jnp.int32, s.shape, 1)
            s = jnp.where(col <= row, s, NEG)
        m_new = jnp.maximum(m, jnp.max(s, axis=-1, keepdims=True))
        alpha = jnp.exp2(m - m_new)
        p = jnp.exp2(s - m_new)
        l = alpha * l + jnp.sum(p, axis=-1, keepdims=True)
        pv = jnp.dot(p.astype(BF16), v_ref[pl.ds(k0, tq), :], preferred_element_type=F32)
        return m_new, l, alpha * acc + pv

    init = (jnp.full((tq, 1), NEG, F32), jnp.zeros((tq, 1), F32), jnp.zeros((tq, MLA_V), F32))
    carry = lax.fori_loop(
        0, qi, lambda i, c: step(pl.multiple_of(i * tq, tq), c, False), init)
    m, l, acc = step(pl.multiple_of(qi * tq, tq), carry, True)
    o_ref[...] = (acc / l).astype(o_ref.dtype)


def _mla(qn, qr, kn, kr2, v, batch, seq):
    tq = 512
    nq = seq // tq
    t = batch * seq
    kv_spec = pl.BlockSpec((seq, LANES), lambda b, h, i: (b, h))
    return pl.pallas_call(
        functools.partial(_mla_kernel, tq=tq),
        grid=(batch, MLA_HEADS, nq),
        in_specs=[
            pl.BlockSpec((tq, LANES), lambda b, h, i: (b * nq + i, h)),
            pl.BlockSpec((tq, LANES), lambda b, h, i: (b * nq + i, h // 2)),
            kv_spec,
            pl.BlockSpec((seq, LANES), lambda b, h, i: (b, 0)),
            kv_spec,
        ],
        out_specs=pl.BlockSpec((tq, LANES), lambda b, h, i: (b * nq + i, h)),
        out_shape=jax.ShapeDtypeStruct((t, MLA_HEADS * MLA_V), BF16),
        compiler_params=_cparams(("parallel", "parallel", "arbitrary")),
        name="mla",
    )(qn, qr, kn, kr2, v)


def _t5_bucket(dist):
    max_exact = REL_BUCKETS // 2
    n = jnp.maximum(dist, 0)
    large = max_exact + (jnp.log(jnp.maximum(n, 1).astype(F32) / max_exact)
                         / math.log(REL_MAX_DIST / max_exact)
                         * (REL_BUCKETS - max_exact)).astype(jnp.int32)
    large = jnp.minimum(large, REL_BUCKETS - 1)
    return jnp.where(n < max_exact, n, large)


def _t5bias_kernel(rb_ref, bucket_ref, o_ref):
    head = pl.program_id(0)
    bucket = bucket_ref[...]
    acc = jnp.zeros(bucket.shape, F32)
    for b in range(REL_BUCKETS):
        acc = jnp.where(bucket == b, rb_ref[b, head], acc)
    o_ref[...] = acc


def _t5bias(rel_bias):
    a = jnp.arange(BLOCK)
    bidx = jnp.arange(2 * BLOCK)
    bucket = _t5_bucket(BLOCK + a[:, None] - bidx[None, :]).astype(jnp.int32)
    return pl.pallas_call(
        _t5bias_kernel,
        grid=(SWA_HEADS,),
        in_specs=[
            pl.BlockSpec(memory_space=pltpu.SMEM),
            pl.BlockSpec((BLOCK, 2 * BLOCK), lambda h: (0, 0)),
        ],
        out_specs=pl.BlockSpec((None, BLOCK, 2 * BLOCK), lambda h: (h, 0, 0)),
        out_shape=jax.ShapeDtypeStruct((SWA_HEADS, BLOCK, 2 * BLOCK), F32),
        compiler_params=_cparams(("arbitrary",)),
        name="t5bias",
    )(rel_bias, bucket)


def _swa_kernel(sink_ref, q_ref, kp_ref, kc_ref, vp_ref, vc_ref, bias_ref, o_ref):
    n = pl.program_id(1)
    q = q_ref[...]
    k = jnp.concatenate([kp_ref[...], kc_ref[...]], axis=0)
    v = jnp.concatenate([vp_ref[...], vc_ref[...]], axis=0)
    row = lax.broadcasted_iota(jnp.int32, (BLOCK, 2 * BLOCK), 0)
    col = lax.broadcasted_iota(jnp.int32, (BLOCK, 2 * BLOCK), 1)
    dist = BLOCK + row - col
    first_col = jnp.where(n > 0, 0, BLOCK)
    mask = (dist >= 0) & (dist < WINDOW) & (col >= first_col)
    outs = []
    for hq in range(SWA_HEADS):
        kvh = hq // SWA_GROUP
        lo = SWA_HEAD_DIM * kvh
        k_h = k[:, lo:lo + SWA_HEAD_DIM]
        v_h = v[:, lo:lo + SWA_HEAD_DIM]
        q_h = q[:, SWA_HEAD_DIM * hq:SWA_HEAD_DIM * (hq + 1)]
        s = lax.dot_general(q_h, k_h, (((1,), (1,)), ((), ())), preferred_element_type=F32)
        s = jnp.where(mask, s + bias_ref[hq], NEG)
        sink = sink_ref[hq]
        m = jnp.maximum(jnp.max(s, axis=-1, keepdims=True), sink)
        e = jnp.exp(s - m)
        denom = jnp.sum(e, axis=-1, keepdims=True) + jnp.exp(sink - m)
        o = jnp.dot(e.astype(BF16), v_h, preferred_element_type=F32)
        outs.append((o / denom).astype(BF16))
    o_ref[...] = jnp.concatenate(outs, axis=1)


def _swa(z, bias, sinks, batch, seq):
    nb = seq // BLOCK
    t = batch * seq
    qcols = SWA_HEADS * SWA_HEAD_DIM

    def cur(col):
        return pl.BlockSpec((BLOCK, SWA_KV), lambda b, n: (b * nb + n, col))

    def prev(col):
        return pl.BlockSpec((BLOCK, SWA_KV), lambda b, n: (b * nb + jnp.maximum(n - 1, 0), col))

    return pl.pallas_call(
        _swa_kernel,
        grid=(batch, nb),
        in_specs=[
            pl.BlockSpec(memory_space=pltpu.SMEM),
            pl.BlockSpec((BLOCK, qcols), lambda b, n: (b * nb + n, Z_QS // qcols)),
            prev(Z_KS // SWA_KV), cur(Z_KS // SWA_KV),
            prev(Z_VS // SWA_KV), cur(Z_VS // SWA_KV),
            pl.BlockSpec((SWA_HEADS, BLOCK, 2 * BLOCK), lambda b, n: (0, 0, 0)),
        ],
        out_specs=pl.BlockSpec((BLOCK, qcols), lambda b, n: (b * nb + n, 0)),
        out_shape=jax.ShapeDtypeStruct((t, qcols), BF16),
        compiler_params=_cparams(("parallel", "arbitrary")),
        name="swa",
    )(sinks, z, z, z, z, z, bias)


def _mix_kernel(oa_ref, ob_ref, ga_ref, gb_ref, w_ref, x_ref, g_ref, mod_ref, o_ref):
    ga = jax.nn.sigmoid(ga_ref[...].astype(F32))
    gb = jax.nn.sigmoid(gb_ref[...].astype(F32))
    u = ga * oa_ref[...].astype(F32) + gb * ob_ref[...].astype(F32)
    mix = jnp.dot(u.astype(BF16), w_ref[...], preferred_element_type=F32)
    o_ref[...] = x_ref[...] + mod_ref[GT1:GT1 + 1, :] * _rms(mix, g_ref[...])


def _mix(oa, ob, z, w, x2, g, mod, seq):
    t, d = x2.shape
    tm = 256
    tpb = seq // tm
    row = lambda i: (i, 0)
    return pl.pallas_call(
        _mix_kernel,
        grid=(t // tm,),
        in_specs=[
            pl.BlockSpec((tm, d), row),
            pl.BlockSpec((tm, d), row),
            pl.BlockSpec((tm, d), lambda i: (i, Z_GA // d)),
            pl.BlockSpec((tm, d), lambda i: (i, Z_GB // d)),
            pl.BlockSpec((d, d), lambda i: (0, 0)),
            pl.BlockSpec((tm, d), row),
            pl.BlockSpec((1, d), lambda i: (0, 0)),
            pl.BlockSpec((None, 6, d), lambda i: (i // tpb, 0, 0)),
        ],
        out_specs=pl.BlockSpec((tm, d), row),
        out_shape=jax.ShapeDtypeStruct((t, d), F32),
        compiler_params=_cparams(("parallel",)),
        name="mix",
    )(oa, ob, z, z, w, x2, g, mod)


def _ffn_up_kernel(x_ref, halo_ref, g_ref, mod_ref, wg_ref, wv_ref, cwg_ref, cwv_ref,
                   cbg_ref, cbv_ref, o_ref, h_ref, *, tm, tpb):
    i = pl.program_id(0)

    @pl.when(pl.program_id(1) == 0)
    def _():
        g = g_ref[...]
        sc = mod_ref[SC2:SC2 + 1, :]
        sh = mod_ref[SH2:SH2 + 1, :]
        halo = (_rms(halo_ref[...], g) * (1.0 + sc) + sh).astype(BF16)
        h_ref[0:BF16_ROWS, :] = jnp.where(i % tpb == 0, jnp.zeros_like(halo), halo)
        _fill_normed(x_ref, g, sc, sh, h_ref, BF16_ROWS, tm, 128)

    h = h_ref[...]

    def conv(w_ref, cw_ref, cb_ref):
        u = jnp.dot(h, w_ref[...], preferred_element_type=F32)
        y = cb_ref[...] + cw_ref[0:1, :] * u[BF16_ROWS - 2:BF16_ROWS - 2 + tm, :]
        y = y + cw_ref[1:2, :] * u[BF16_ROWS - 1:BF16_ROWS - 1 + tm, :]
        return y + cw_ref[2:3, :] * u[BF16_ROWS:BF16_ROWS + tm, :]

    gate = conv(wg_ref, cwg_ref, cbg_ref)
    val = conv(wv_ref, cwv_ref, cbv_ref)
    o_ref[...] = (gate * jax.nn.sigmoid(gate) * val).astype(o_ref.dtype)


def _ffn_up(x1, g, mod, w_up, conv_w, conv_b, seq):
    t, d = x1.shape
    tm, tn = 512, 512
    tpb = seq // tm
    nj = D_FF // tn
    hb = tm // BF16_ROWS
    return pl.pallas_call(
        functools.partial(_ffn_up_kernel, tm=tm, tpb=tpb),
        grid=(t // tm, nj),
        in_specs=[
            pl.BlockSpec((tm, d), lambda i, j: (i, 0)),
            pl.BlockSpec((BF16_ROWS, d), lambda i, j: (jnp.maximum(i * hb - 1, 0), 0)),
            pl.BlockSpec((1, d), lambda i, j: (0, 0)),
            pl.BlockSpec((None, 6, d), lambda i, j: (i // tpb, 0, 0)),
            pl.BlockSpec((d, tn), lambda i, j: (0, j)),
            pl.BlockSpec((d, tn), lambda i, j: (0, j + nj)),
            pl.BlockSpec((CONV_WIDTH, tn), lambda i, j: (0, j)),
            pl.BlockSpec((CONV_WIDTH, tn), lambda i, j: (0, j + nj)),
            pl.BlockSpec((1, tn), lambda i, j: (0, j)),
            pl.BlockSpec((1, tn), lambda i, j: (0, j + nj)),
        ],
        out_specs=pl.BlockSpec((tm, tn), lambda i, j: (i, j)),
        out_shape=jax.ShapeDtypeStruct((t, D_FF), BF16),
        scratch_shapes=[pltpu.VMEM((tm + BF16_ROWS, d), BF16)],
        compiler_params=_cparams(("parallel", "arbitrary")),
        name="ffn_up",
    )(x1, x1, g, mod, w_up, w_up, conv_w, conv_w, conv_b, conv_b)


def _ffn_down_kernel(a_ref, w_ref, x_ref, g_ref, mod_ref, o_ref, acc_ref, *, nk):
    k = pl.program_id(1)

    @pl.when(k == 0)
    def _():
        acc_ref[...] = jnp.zeros_like(acc_ref)

    acc_ref[...] += jnp.dot(a_ref[...], w_ref[...], preferred_element_type=F32)

    @pl.when(k == nk - 1)
    def _():
        o_ref[...] = x_ref[...] + mod_ref[GT2:GT2 + 1, :] * _rms(acc_ref[...], g_ref[...])


def _ffn_down(a, w, x1, g, mod, seq):
    t, d = x1.shape
    tm, tk = 512, 1408
    nk = D_FF // tk
    tpb = seq // tm
    return pl.pallas_call(
        functools.partial(_ffn_down_kernel, nk=nk),
        grid=(t // tm, nk),
        in_specs=[
            pl.BlockSpec((tm, tk), lambda i, k: (i, k)),
            pl.BlockSpec((tk, d), lambda i, k: (k, 0)),
            pl.BlockSpec((tm, d), lambda i, k: (i, 0)),
            pl.BlockSpec((1, d), lambda i, k: (0, 0)),
            pl.BlockSpec((None, 6, d), lambda i, k: (i // tpb, 0, 0)),
        ],
        out_specs=pl.BlockSpec((tm, d), lambda i, k: (i, 0)),
        out_shape=jax.ShapeDtypeStruct((t, d), F32),
        scratch_shapes=[pltpu.VMEM((tm, d), F32)],
        compiler_params=_cparams(("parallel", "arbitrary")),
        name="ffn_down",
    )(a, w, x1, g, mod)


def _rot_cols(w):
    half = MLA_ROPE // 2
    w = w.reshape(w.shape[0], -1, MLA_ROPE)
    return jnp.concatenate([-w[..., half:], w[..., :half]], axis=-1).reshape(w.shape[0], -1)


def _prep_w_in(w_in):
    o = 0
    cq = w_in[:, o:o + MLA_Q_RANK]; o += MLA_Q_RANK
    ckv = w_in[:, o:o + MLA_KV_RANK]; o += MLA_KV_RANK
    kr = w_in[:, o:o + MLA_ROPE]; o += MLA_ROPE
    qs = w_in[:, o:o + D_MODEL]; o += D_MODEL
    ks = w_in[:, o:o + SWA_KV]; o += SWA_KV
    vs = w_in[:, o:o + SWA_KV]; o += SWA_KV
    gates = w_in[:, o:o + 2 * D_MODEL]
    krr = _rot_cols(kr)
    w = jnp.concatenate([qs * SWA_HEAD_DIM ** -0.5, gates, cq, ks, ckv, vs, kr, kr, krr, krr],
                        axis=1)
    assert w.shape[1] == Z_COLS
    return w.astype(BF16)


def _prep_w_uq(w_uq):
    w = (w_uq * (MLA_QK ** -0.5 * LOG2E)).reshape(MLA_Q_RANK, MLA_HEADS, MLA_QK)
    wn = w[..., :MLA_NOPE].reshape(MLA_Q_RANK, -1)
    wr = w[..., MLA_NOPE:].reshape(MLA_Q_RANK, -1)
    return wn.astype(BF16), wr.astype(BF16), _rot_cols(wr).astype(BF16)


def _prep_w_ukv(w_ukv):
    w = w_ukv.reshape(MLA_KV_RANK, MLA_HEADS, MLA_NOPE + MLA_V)
    wk = w[..., :MLA_NOPE].reshape(MLA_KV_RANK, -1)
    wv = w[..., MLA_NOPE:].reshape(MLA_KV_RANK, -1)
    return wk.astype(BF16), wv.astype(BF16)


def _rope_tables2(seq):
    pos = jnp.arange(seq, dtype=F32)
    inv = ROPE_THETA ** (-jnp.arange(0, MLA_ROPE, 2, dtype=F32) / MLA_ROPE)
    ang = pos[:, None] * inv[None, :]
    ang = jnp.concatenate([ang, ang, ang, ang], axis=-1)
    return jnp.cos(ang), jnp.sin(ang)


def kernel(x, c, w_ada, b_ada, g_pre_mix, g_post_mix, w_in, g_q_lat, w_uq, g_kv_lat, w_ukv,
           rel_bias, sinks, w_o, g_pre_ffn, g_post_ffn, w_up, conv_w, conv_b, w_down):
    batch, seq, d = x.shape
    depth = w_ada.shape[0]
    xt = x.reshape(batch * seq, d)
    c8 = jnp.pad(c, ((0, 8 - batch), (0, 0)))
    cos2, sin2 = _rope_tables2(seq)
    bias = _t5bias(rel_bias)
    for l in range(depth):
        mod = _ada(c8, w_ada[l], b_ada[l][None, :])[:batch].reshape(batch, 6, d)
        z = _inproj(xt, g_pre_mix[l][None, :], mod, _prep_w_in(w_in[l]), seq)
        wn, wr, wrr = _prep_w_uq(w_uq[l])
        qn, qr = _qproj(z, g_q_lat[l][None, :], wn, wr, wrr, cos2, sin2, seq)
        wk, wv = _prep_w_ukv(w_ukv[l])
        kn, v, kr2 = _kvproj(z, g_kv_lat[l][None, :], wk, wv, cos2, sin2, seq)
        o_a = _mla(qn, qr, kn, kr2, v, batch, seq)
        o_b = _swa(z, bias, sinks[l], batch, seq)
        x1 = _mix(o_a, o_b, z, w_o[l].astype(BF16), xt, g_post_mix[l][None, :], mod, seq)
        a = _ffn_up(x1, g_pre_ffn[l][None, :], mod, w_up[l].astype(BF16), conv_w[l],
                    conv_b[l][None, :], seq)
        xt = _ffn_down(a, w_down[l].astype(BF16), x1, g_post_ffn[l][None, :], mod, seq)
    return xt.reshape(batch, seq, d)
```

```python
import functools
import math

import jax
import jax.numpy as jnp
from jax import lax
from jax.experimental import pallas as pl
from jax.experimental.pallas import tpu as pltpu

F32 = jnp.float32
BF16 = jnp.bfloat16

D_MODEL = 2048
MLA_NOPE = 128
MLA_ROPE = 64
MLA_V = 128
MLA_HEADS = D_MODEL // MLA_V
MLA_Q_RANK = 768
MLA_KV_RANK = 512
MLA_QK = MLA_NOPE + MLA_ROPE
ROPE_THETA = 10000.0
SWA_HEAD_DIM = 64
SWA_HEADS = D_MODEL // SWA_HEAD_DIM
SWA_KV_HEADS = 4
SWA_GROUP = SWA_HEADS // SWA_KV_HEADS
SWA_KV = SWA_KV_HEADS * SWA_HEAD_DIM
WINDOW = 128
BLOCK = 128
REL_BUCKETS = 32
REL_MAX_DIST = 128
D_FF = 5632
CONV_WIDTH = 3
EPS = 1e-6
NEG = -1e30
LOG2E = 1.4426950408889634

VMEM_LIMIT_BYTES = 56 * 1024 * 1024
LANES = 128
BF16_ROWS = 16
MLA_TILE = 512

Z_QS = 0
Z_GA = 2048
Z_GB = 4096
Z_CQ = 6144
Z_KS = 6912
Z_CKV = 7168
Z_VS = 7680
Z_KR = 7936
Z_KRR = 8064
Z_COLS = 8192

SH1, SC1, GT1, SH2, SC2, GT2 = range(6)


def _cparams(sem):
    return pltpu.CompilerParams(dimension_semantics=sem, vmem_limit_bytes=VMEM_LIMIT_BYTES)


def _rms(x, g):
    ms = jnp.mean(x * x, axis=-1, keepdims=True)
    return x * lax.rsqrt(ms + EPS) * g


def _ada_kernel(c_ref, w_ref, b_ref, o_ref):
    c = c_ref[...]
    ca = (c * jax.nn.sigmoid(c)).astype(BF16)
    o_ref[...] = jnp.dot(ca, w_ref[...].astype(BF16), preferred_element_type=F32) + b_ref[...]


def _ada(c8, w_ada, b_ada):
    n = w_ada.shape[1]
    tn = 1024
    return pl.pallas_call(
        _ada_kernel,
        grid=(n // tn,),
        in_specs=[
            pl.BlockSpec((8, D_MODEL), lambda j: (0, 0)),
            pl.BlockSpec((D_MODEL, tn), lambda j: (0, j)),
            pl.BlockSpec((1, tn), lambda j: (0, j)),
        ],
        out_specs=pl.BlockSpec((8, tn), lambda j: (0, j)),
        out_shape=jax.ShapeDtypeStruct((8, n), F32),
        compiler_params=_cparams(("arbitrary",)),
        name="ada",
    )(c8, w_ada, b_ada)


def _fill_normed(x_ref, g, sc, sh, h_ref, row0, nrows, chunk):
    def body(i, carry):
        r = pl.multiple_of(i * chunk, chunk)
        y = _rms(x_ref[pl.ds(r, chunk), :], g)
        h_ref[pl.ds(row0 + r, chunk), :] = (y * (1.0 + sc) + sh).astype(BF16)
        return carry

    lax.fori_loop(0, nrows // chunk, body, 0)


def _inproj_kernel(x_ref, g_ref, mod_ref, w_ref, o_ref, h_ref, *, tm):
    @pl.when(pl.program_id(1) == 0)
    def _():
        _fill_normed(x_ref, g_ref[...], mod_ref[SC1:SC1 + 1, :], mod_ref[SH1:SH1 + 1, :],
                     h_ref, 0, tm, 128)

    o_ref[...] = jnp.dot(h_ref[...], w_ref[...], preferred_element_type=F32).astype(o_ref.dtype)


def _inproj(x2, g, mod, w, seq):
    t, d = x2.shape
    n = w.shape[1]
    tm, tn = 1024, 1024
    tpb = seq // tm
    return pl.pallas_call(
        functools.partial(_inproj_kernel, tm=tm),
        grid=(t // tm, n // tn),
        in_specs=[
            pl.BlockSpec((tm, d), lambda i, j: (i, 0)),
            pl.BlockSpec((1, d), lambda i, j: (0, 0)),
            pl.BlockSpec((None, 6, d), lambda i, j: (i // tpb, 0, 0)),
            pl.BlockSpec((d, tn), lambda i, j: (0, j)),
        ],
        out_specs=pl.BlockSpec((tm, tn), lambda i, j: (i, j)),
        out_shape=jax.ShapeDtypeStruct((t, n), BF16),
        scratch_shapes=[pltpu.VMEM((tm, d), BF16)],
        compiler_params=_cparams(("parallel", "arbitrary")),
        name="inproj",
    )(x2, g, mod, w)


def _rope_chunks(a, b, cos, sin):
    n = a.shape[1] // LANES
    out = [a[:, LANES * i:LANES * (i + 1)] * cos + b[:, LANES * i:LANES * (i + 1)] * sin
           for i in range(n)]
    return out[0] if n == 1 else jnp.concatenate(out, axis=1)


def _qproj_kernel(cq_ref, g_ref, wn_ref, wr_ref, wrr_ref, cos_ref, sin_ref, qn_ref, qr_ref):
    cqn = _rms(cq_ref[...].astype(F32), g_ref[...]).astype(BF16)
    qn_ref[...] = jnp.dot(cqn, wn_ref[...], preferred_element_type=F32).astype(BF16)
    a = jnp.dot(cqn, wr_ref[...], preferred_element_type=F32)
    b = jnp.dot(cqn, wrr_ref[...], preferred_element_type=F32)
    qr_ref[...] = _rope_chunks(a, b, cos_ref[...], sin_ref[...]).astype(BF16)


def _qproj(z, g, wn, wr, wrr, cos2, sin2, seq):
    t = z.shape[0]
    tm = 512
    spb = seq // tm
    hn = MLA_HEADS * MLA_NOPE
    hr = MLA_HEADS * MLA_ROPE
    return pl.pallas_call(
        _qproj_kernel,
        grid=(t // tm,),
        in_specs=[
            pl.BlockSpec((tm, MLA_Q_RANK), lambda i: (i, Z_CQ // MLA_Q_RANK)),
            pl.BlockSpec((1, MLA_Q_RANK), lambda i: (0, 0)),
            pl.BlockSpec((MLA_Q_RANK, hn), lambda i: (0, 0)),
            pl.BlockSpec((MLA_Q_RANK, hr), lambda i: (0, 0)),
            pl.BlockSpec((MLA_Q_RANK, hr), lambda i: (0, 0)),
            pl.BlockSpec((tm, LANES), lambda i: (i % spb, 0)),
            pl.BlockSpec((tm, LANES), lambda i: (i % spb, 0)),
        ],
        out_specs=[
            pl.BlockSpec((tm, hn), lambda i: (i, 0)),
            pl.BlockSpec((tm, hr), lambda i: (i, 0)),
        ],
        out_shape=[jax.ShapeDtypeStruct((t, hn), BF16), jax.ShapeDtypeStruct((t, hr), BF16)],
        compiler_params=_cparams(("parallel",)),
        name="qproj",
    )(z, g, wn, wr, wrr, cos2, sin2)


def _kvproj_kernel(ckv_ref, g_ref, wk_ref, wvt_ref, kr_ref, krr_ref, cos_ref, sin_ref,
                   kn_ref, vt_ref, kro_ref):
    ckvn = _rms(ckv_ref[...].astype(F32), g_ref[...]).astype(BF16)
    kn_ref[...] = jnp.dot(ckvn, wk_ref[...], preferred_element_type=F32).astype(BF16)
    vt = lax.dot_general(wvt_ref[...], ckvn, (((1,), (1,)), ((), ())), preferred_element_type=F32)
    vt_ref[...] = vt.reshape(vt_ref.shape).astype(BF16)
    kro_ref[...] = (kr_ref[...].astype(F32) * cos_ref[...]
                    + krr_ref[...].astype(F32) * sin_ref[...]).astype(BF16)


def _kvproj(z, g, wk, wvt, cos2, sin2, batch, seq):
    t = z.shape[0]
    tm = MLA_TILE
    spb = seq // tm
    hn = MLA_HEADS * MLA_NOPE
    hv = MLA_HEADS * MLA_V
    return pl.pallas_call(
        _kvproj_kernel,
        grid=(t // tm,),
        in_specs=[
            pl.BlockSpec((tm, MLA_KV_RANK), lambda i: (i, Z_CKV // MLA_KV_RANK)),
            pl.BlockSpec((1, MLA_KV_RANK), lambda i: (0, 0)),
            pl.BlockSpec((MLA_KV_RANK, hn), lambda i: (0, 0)),
            pl.BlockSpec((hv, MLA_KV_RANK), lambda i: (0, 0)),
            pl.BlockSpec((tm, LANES), lambda i: (i, Z_KR // LANES)),
            pl.BlockSpec((tm, LANES), lambda i: (i, Z_KRR // LANES)),
            pl.BlockSpec((tm, LANES), lambda i: (i % spb, 0)),
            pl.BlockSpec((tm, LANES), lambda i: (i % spb, 0)),
        ],
        out_specs=[
            pl.BlockSpec((tm, hn), lambda i: (i, 0)),
            pl.BlockSpec((None, MLA_HEADS, None, MLA_V, tm), lambda i: (i // spb, 0, i % spb, 0, 0)),
            pl.BlockSpec((tm, LANES), lambda i: (i, 0)),
        ],
        out_shape=[jax.ShapeDtypeStruct((t, hn), BF16),
                   jax.ShapeDtypeStruct((batch, MLA_HEADS, spb, MLA_V, tm), BF16),
                   jax.ShapeDtypeStruct((t, LANES), BF16)],
        compiler_params=_cparams(("parallel",)),
        name="kvproj",
    )(z, g, wk, wvt, z, z, cos2, sin2)


def _mla_kernel(qn_ref, qr_ref, kn_ref, kr_ref, vt_ref, o_ref, q_ref, s_ref, p_ref, acc_ref,
                *, tile, nq):
    head = pl.program_id(1)
    lo = (head % 2) * MLA_ROPE

    def scores(q, k):
        k0 = k * tile if isinstance(k, int) else pl.multiple_of(k * tile, tile)
        kc = jnp.concatenate([kn_ref[pl.ds(k0, tile), :], kr_ref[pl.ds(k0, tile), :]], axis=1)
        return lax.dot_general(kc, q, (((1,), (1,)), ((), ())), preferred_element_type=F32)

    def softmax(s, m, l):
        m_new = jnp.maximum(m, jnp.max(s, axis=0, keepdims=True))
        alpha = jnp.exp2(m - m_new)
        p = jnp.exp2(s - m_new)
        return p.astype(BF16), alpha, m_new, alpha * l + jnp.sum(p, axis=0, keepdims=True)

    def values(p, k):
        return jnp.dot(vt_ref[k], p, preferred_element_type=F32)

    for qi in range(nq):
        rows = slice(qi * tile, (qi + 1) * tile)
        qr = qr_ref[rows, :]
        lane = lax.broadcasted_iota(jnp.int32, qr.shape, 1)
        keep = (lane >= lo) & (lane < lo + MLA_ROPE)
        q_ref[...] = jnp.concatenate(
            [qn_ref[rows, :], jnp.where(keep, qr, jnp.zeros_like(qr))], axis=1)

        s_ref[0] = scores(q_ref[...], 0)
        m = jnp.full((1, tile), NEG, F32)
        l = jnp.zeros((1, tile), F32)
        for k in range(qi + 1):
            if k < qi:
                s_ref[(k + 1) % 3] = scores(q_ref[...], k + 1)
            s = s_ref[k % 3]
            if k == qi:
                key = lax.broadcasted_iota(jnp.int32, s.shape, 0)
                qry = lax.broadcasted_iota(jnp.int32, s.shape, 1)
                s = jnp.where(key <= qry, s, NEG)
            p, a, m, l = softmax(s, m, l)
            p_ref[k % 2] = p
            pv = values(p_ref[k % 2], k)
            acc_ref[...] = pv if k == 0 else a * acc_ref[...] + pv
        o_ref[rows, :] = (acc_ref[...] / l).T.astype(o_ref.dtype)


def _mla(qn, qr, kn, kr2, vt, batch, seq):
    tile = MLA_TILE
    nq = seq // tile
    t = batch * seq
    tok = lambda b, h: (b, h)
    return pl.pallas_call(
        functools.partial(_mla_kernel, tile=tile, nq=nq),
        grid=(batch, MLA_HEADS),
        in_specs=[
            pl.BlockSpec((seq, LANES), tok),
            pl.BlockSpec((seq, LANES), lambda b, h: (b, h // 2)),
            pl.BlockSpec((seq, LANES), tok),
            pl.BlockSpec((seq, LANES), lambda b, h: (b, 0)),
            pl.BlockSpec((None, None, nq, MLA_V, tile), lambda b, h: (b, h, 0, 0, 0)),
        ],
        out_specs=pl.BlockSpec((seq, LANES), tok),
        out_shape=jax.ShapeDtypeStruct((t, MLA_HEADS * MLA_V), BF16),
        scratch_shapes=[
            pltpu.VMEM((tile, 2 * LANES), BF16),
            pltpu.VMEM((3, tile, tile), F32),
            pltpu.VMEM((2, tile, tile), BF16),
            pltpu.VMEM((MLA_V, tile), F32),
        ],
        compiler_params=_cparams(("parallel", "arbitrary")),
        name="mla",
    )(qn, qr, kn, kr2, vt)


def _t5_bucket(dist):
    max_exact = REL_BUCKETS // 2
    n = jnp.maximum(dist, 0)
    large = max_exact + (jnp.log(jnp.maximum(n, 1).astype(F32) / max_exact)
                         / math.log(REL_MAX_DIST / max_exact)
                         * (REL_BUCKETS - max_exact)).astype(jnp.int32)
    large = jnp.minimum(large, REL_BUCKETS - 1)
    return jnp.where(n < max_exact, n, large)


def _t5bias_kernel(rb_ref, bucket_ref, o_ref):
    head = pl.program_id(0)
    bucket = bucket_ref[...]
    acc = jnp.zeros(bucket.shape, F32)
    for b in range(REL_BUCKETS):
        acc = jnp.where(bucket == b, rb_ref[b, head], acc)
    o_ref[...] = acc


def _t5bias(rel_bias):
    a = jnp.arange(BLOCK)
    bidx = jnp.arange(2 * BLOCK)
    bucket = _t5_bucket(BLOCK + a[:, None] - bidx[None, :]).astype(jnp.int32)
    return pl.pallas_call(
        _t5bias_kernel,
        grid=(SWA_HEADS,),
        in_specs=[
            pl.BlockSpec(memory_space=pltpu.SMEM),
            pl.BlockSpec((BLOCK, 2 * BLOCK), lambda h: (0, 0)),
        ],
        out_specs=pl.BlockSpec((None, BLOCK, 2 * BLOCK), lambda h: (h, 0, 0)),
        out_shape=jax.ShapeDtypeStruct((SWA_HEADS, BLOCK, 2 * BLOCK), F32),
        compiler_params=_cparams(("arbitrary",)),
        name="t5bias",
    )(rel_bias, bucket)


def _swa_kernel(sink_ref, q_ref, kp_ref, kc_ref, vp_ref, vc_ref, bias_ref, o_ref):
    n = pl.program_id(1)
    q = q_ref[...]
    k = jnp.concatenate([kp_ref[...], kc_ref[...]], axis=0)
    v = jnp.concatenate([vp_ref[...], vc_ref[...]], axis=0)
    row = lax.broadcasted_iota(jnp.int32, (BLOCK, 2 * BLOCK), 0)
    col = lax.broadcasted_iota(jnp.int32, (BLOCK, 2 * BLOCK), 1)
    dist = BLOCK + row - col
    first_col = jnp.where(n > 0, 0, BLOCK)
    mask = (dist >= 0) & (dist < WINDOW) & (col >= first_col)
    outs = []
    for hq in range(SWA_HEADS):
        kvh = hq // SWA_GROUP
        lo = SWA_HEAD_DIM * kvh
        k_h = k[:, lo:lo + SWA_HEAD_DIM]
        v_h = v[:, lo:lo + SWA_HEAD_DIM]
        q_h = q[:, SWA_HEAD_DIM * hq:SWA_HEAD_DIM * (hq + 1)]
        s = lax.dot_general(q_h, k_h, (((1,), (1,)), ((), ())), preferred_element_type=F32)
        s = jnp.where(mask, s + bias_ref[hq], NEG)
        sink = sink_ref[hq]
        m = jnp.maximum(jnp.max(s, axis=-1, keepdims=True), sink)
        e = jnp.exp(s - m)
        denom = jnp.sum(e, axis=-1, keepdims=True) + jnp.exp(sink - m)
        o = jnp.dot(e.astype(BF16), v_h, preferred_element_type=F32)
        outs.append((o / denom).astype(BF16))
    o_ref[...] = jnp.concatenate(outs, axis=1)


def _swa(z, bias, sinks, batch, seq):
    nb = seq // BLOCK
    t = batch * seq
    qcols = SWA_HEADS * SWA_HEAD_DIM

    def cur(col):
        return pl.BlockSpec((BLOCK, SWA_KV), lambda b, n: (b * nb + n, col))

    def prev(col):
        return pl.BlockSpec((BLOCK, SWA_KV), lambda b, n: (b * nb + jnp.maximum(n - 1, 0), col))

    return pl.pallas_call(
        _swa_kernel,
        grid=(batch, nb),
        in_specs=[
            pl.BlockSpec(memory_space=pltpu.SMEM),
            pl.BlockSpec((BLOCK, qcols), lambda b, n: (b * nb + n, Z_QS // qcols)),
            prev(Z_KS // SWA_KV), cur(Z_KS // SWA_KV),
            prev(Z_VS // SWA_KV), cur(Z_VS // SWA_KV),
            pl.BlockSpec((SWA_HEADS, BLOCK, 2 * BLOCK), lambda b, n: (0, 0, 0)),
        ],
        out_specs=pl.BlockSpec((BLOCK, qcols), lambda b, n: (b * nb + n, 0)),
        out_shape=jax.ShapeDtypeStruct((t, qcols), BF16),
        compiler_params=_cparams(("parallel", "arbitrary")),
        name="swa",
    )(sinks, z, z, z, z, z, bias)


def _mix_kernel(oa_ref, ob_ref, ga_ref, gb_ref, w_ref, x_ref, g_ref, mod_ref, o_ref):
    ga = jax.nn.sigmoid(ga_ref[...].astype(F32))
    gb = jax.nn.sigmoid(gb_ref[...].astype(F32))
    u = ga * oa_ref[...].astype(F32) + gb * ob_ref[...].astype(F32)
    mix = jnp.dot(u.astype(BF16), w_ref[...], preferred_element_type=F32)
    o_ref[...] = x_ref[...] + mod_ref[GT1:GT1 + 1, :] * _rms(mix, g_ref[...])


def _mix(oa, ob, z, w, x2, g, mod, seq):
    t, d = x2.shape
    tm = 256
    tpb = seq // tm
    row = lambda i: (i, 0)
    return pl.pallas_call(
        _mix_kernel,
        grid=(t // tm,),
        in_specs=[
            pl.BlockSpec((tm, d), row),
            pl.BlockSpec((tm, d), row),
            pl.BlockSpec((tm, d), lambda i: (i, Z_GA // d)),
            pl.BlockSpec((tm, d), lambda i: (i, Z_GB // d)),
            pl.BlockSpec((d, d), lambda i: (0, 0)),
            pl.BlockSpec((tm, d), row),
            pl.BlockSpec((1, d), lambda i: (0, 0)),
            pl.BlockSpec((None, 6, d), lambda i: (i // tpb, 0, 0)),
        ],
        out_specs=pl.BlockSpec((tm, d), row),
        out_shape=jax.ShapeDtypeStruct((t, d), F32),
        compiler_params=_cparams(("parallel",)),
        name="mix",
    )(oa, ob, z, z, w, x2, g, mod)


def _ffn_up_kernel(x_ref, halo_ref, g_ref, mod_ref, wg_ref, wv_ref, cwg_ref, cwv_ref,
                   cbg_ref, cbv_ref, o_ref, h_ref, *, tm, tpb):
    i = pl.program_id(0)

    @pl.when(pl.program_id(1) == 0)
    def _():
        g = g_ref[...]
        sc = mod_ref[SC2:SC2 + 1, :]
        sh = mod_ref[SH2:SH2 + 1, :]
        halo = (_rms(halo_ref[...], g) * (1.0 + sc) + sh).astype(BF16)
        h_ref[0:BF16_ROWS, :] = jnp.where(i % tpb == 0, jnp.zeros_like(halo), halo)
        _fill_normed(x_ref, g, sc, sh, h_ref, BF16_ROWS, tm, 128)

    h = h_ref[...]

    def conv(w_ref, cw_ref, cb_ref):
        u = jnp.dot(h, w_ref[...], preferred_element_type=F32)
        y = cb_ref[...] + cw_ref[0:1, :] * u[BF16_ROWS - 2:BF16_ROWS - 2 + tm, :]
        y = y + cw_ref[1:2, :] * u[BF16_ROWS - 1:BF16_ROWS - 1 + tm, :]
        return y + cw_ref[2:3, :] * u[BF16_ROWS:BF16_ROWS + tm, :]

    gate = conv(wg_ref, cwg_ref, cbg_ref)
    val = conv(wv_ref, cwv_ref, cbv_ref)
    o_ref[...] = (gate * jax.nn.sigmoid(gate) * val).astype(o_ref.dtype)


def _ffn_up(x1, g, mod, w_up, conv_w, conv_b, seq):
    t, d = x1.shape
    tm, tn = 512, 512
    tpb = seq // tm
    nj = D_FF // tn
    hb = tm // BF16_ROWS
    return pl.pallas_call(
        functools.partial(_ffn_up_kernel, tm=tm, tpb=tpb),
        grid=(t // tm, nj),
        in_specs=[
            pl.BlockSpec((tm, d), lambda i, j: (i, 0)),
            pl.BlockSpec((BF16_ROWS, d), lambda i, j: (jnp.maximum(i * hb - 1, 0), 0)),
            pl.BlockSpec((1, d), lambda i, j: (0, 0)),
            pl.BlockSpec((None, 6, d), lambda i, j: (i // tpb, 0, 0)),
            pl.BlockSpec((d, tn), lambda i, j: (0, j)),
            pl.BlockSpec((d, tn), lambda i, j: (0, j + nj)),
            pl.BlockSpec((CONV_WIDTH, tn), lambda i, j: (0, j)),
            pl.BlockSpec((CONV_WIDTH, tn), lambda i, j: (0, j + nj)),
            pl.BlockSpec((1, tn), lambda i, j: (0, j)),
            pl.BlockSpec((1, tn), lambda i, j: (0, j + nj)),
        ],
        out_specs=pl.BlockSpec((tm, tn), lambda i, j: (i, j)),
        out_shape=jax.ShapeDtypeStruct((t, D_FF), BF16),
        scratch_shapes=[pltpu.VMEM((tm + BF16_ROWS, d), BF16)],
        compiler_params=_cparams(("parallel", "arbitrary")),
        name="ffn_up",
    )(x1, x1, g, mod, w_up, w_up, conv_w, conv_w, conv_b, conv_b)


def _ffn_down_kernel(a_ref, w_ref, x_ref, g_ref, mod_ref, o_ref, acc_ref, *, nk):
    k = pl.program_id(1)

    @pl.when(k == 0)
    def _():
        acc_ref[...] = jnp.zeros_like(acc_ref)

    acc_ref[...] += jnp.dot(a_ref[...], w_ref[...], preferred_element_type=F32)

    @pl.when(k == nk - 1)
    def _():
        o_ref[...] = x_ref[...] + mod_ref[GT2:GT2 + 1, :] * _rms(acc_ref[...], g_ref[...])


def _ffn_down(a, w, x1, g, mod, seq):
    t, d = x1.shape
    tm, tk = 512, 1408
    nk = D_FF // tk
    tpb = seq // tm
    return pl.pallas_call(
        functools.partial(_ffn_down_kernel, nk=nk),
        grid=(t // tm, nk),
        in_specs=[
            pl.BlockSpec((tm, tk), lambda i, k: (i, k)),
            pl.BlockSpec((tk, d), lambda i, k: (k, 0)),
            pl.BlockSpec((tm, d), lambda i, k: (i, 0)),
            pl.BlockSpec((1, d), lambda i, k: (0, 0)),
            pl.BlockSpec((None, 6, d), lambda i, k: (i // tpb, 0, 0)),
        ],
        out_specs=pl.BlockSpec((tm, d), lambda i, k: (i, 0)),
        out_shape=jax.ShapeDtypeStruct((t, d), F32),
        scratch_shapes=[pltpu.VMEM((tm, d), F32)],
        compiler_params=_cparams(("parallel", "arbitrary")),
        name="ffn_down",
    )(a, w, x1, g, mod)


def _rot_cols(w):
    half = MLA_ROPE // 2
    w = w.reshape(w.shape[0], -1, MLA_ROPE)
    return jnp.concatenate([-w[..., half:], w[..., :half]], axis=-1).reshape(w.shape[0], -1)


def _prep_w_in(w_in):
    o = 0
    cq = w_in[:, o:o + MLA_Q_RANK]; o += MLA_Q_RANK
    ckv = w_in[:, o:o + MLA_KV_RANK]; o += MLA_KV_RANK
    kr = w_in[:, o:o + MLA_ROPE]; o += MLA_ROPE
    qs = w_in[:, o:o + D_MODEL]; o += D_MODEL
    ks = w_in[:, o:o + SWA_KV]; o += SWA_KV
    vs = w_in[:, o:o + SWA_KV]; o += SWA_KV
    gates = w_in[:, o:o + 2 * D_MODEL]
    krr = _rot_cols(kr)
    w = jnp.concatenate([qs * SWA_HEAD_DIM ** -0.5, gates, cq, ks, ckv, vs, kr, kr, krr, krr],
                        axis=1)
    assert w.shape[1] == Z_COLS
    return w.astype(BF16)


def _prep_w_uq(w_uq):
    w = (w_uq * (MLA_QK ** -0.5 * LOG2E)).reshape(MLA_Q_RANK, MLA_HEADS, MLA_QK)
    wn = w[..., :MLA_NOPE].reshape(MLA_Q_RANK, -1)
    wr = w[..., MLA_NOPE:].reshape(MLA_Q_RANK, -1)
    return wn.astype(BF16), wr.astype(BF16), _rot_cols(wr).astype(BF16)


def _prep_w_ukv(w_ukv):
    w = w_ukv.reshape(MLA_KV_RANK, MLA_HEADS, MLA_NOPE + MLA_V)
    wk = w[..., :MLA_NOPE].reshape(MLA_KV_RANK, -1)
    wvt = w[..., MLA_NOPE:].reshape(MLA_KV_RANK, -1).T
    return wk.astype(BF16), wvt.astype(BF16)


def _rope_tables2(seq):
    pos = jnp.arange(seq, dtype=F32)
    inv = ROPE_THETA ** (-jnp.arange(0, MLA_ROPE, 2, dtype=F32) / MLA_ROPE)
    ang = pos[:, None] * inv[None, :]
    ang = jnp.concatenate([ang, ang, ang, ang], axis=-1)
    return jnp.cos(ang), jnp.sin(ang)


def kernel(x, c, w_ada, b_ada, g_pre_mix, g_post_mix, w_in, g_q_lat, w_uq, g_kv_lat, w_ukv,
           rel_bias, sinks, w_o, g_pre_ffn, g_post_ffn, w_up, conv_w, conv_b, w_down):
    batch, seq, d = x.shape
    depth = w_ada.shape[0]
    xt = x.reshape(batch * seq, d)
    c8 = jnp.pad(c, ((0, 8 - batch), (0, 0)))
    cos2, sin2 = _rope_tables2(seq)
    bias = _t5bias(rel_bias)
    for l in range(depth):
        mod = _ada(c8, w_ada[l], b_ada[l][None, :])[:batch].reshape(batch, 6, d)
        z = _inproj(xt, g_pre_mix[l][None, :], mod, _prep_w_in(w_in[l]), seq)
        wn, wr, wrr = _prep_w_uq(w_uq[l])
        qn, qr = _qproj(z, g_q_lat[l][None, :], wn, wr, wrr, cos2, sin2, seq)
        wk, wvt = _prep_w_ukv(w_ukv[l])
        kn, vt, kr2 = _kvproj(z, g_kv_lat[l][None, :], wk, wvt, cos2, sin2, batch, seq)
        o_a = _mla(qn, qr, kn, kr2, vt, batch, seq)
        o_b = _swa(z, bias, sinks[l], batch, seq)
        x1 = _mix(o_a, o_b, z, w_o[l].astype(BF16), xt, g_post_mix[l][None, :], mod, seq)
        a = _ffn_up(x1, g_pre_ffn[l][None, :], mod, w_up[l].astype(BF16), conv_w[l],
                    conv_b[l][None, :], seq)
        xt = _ffn_down(a, w_down[l].astype(BF16), x1, g_post_ffn[l][None, :], mod, seq)
    return xt.reshape(batch, seq, d)
```

```python
import functools
import math

import jax
import jax.numpy as jnp
from jax import lax
from jax.experimental import pallas as pl
from jax.experimental.pallas import tpu as pltpu

F32 = jnp.float32
BF16 = jnp.bfloat16

D_MODEL = 2048
MLA_NOPE = 128
MLA_ROPE = 64
MLA_V = 128
MLA_HEADS = D_MODEL // MLA_V
MLA_Q_RANK = 768
MLA_KV_RANK = 512
MLA_QK = MLA_NOPE + MLA_ROPE
ROPE_THETA = 10000.0
SWA_HEAD_DIM = 64
SWA_HEADS = D_MODEL // SWA_HEAD_DIM
SWA_KV_HEADS = 4
SWA_GROUP = SWA_HEADS // SWA_KV_HEADS
SWA_KV = SWA_KV_HEADS * SWA_HEAD_DIM
WINDOW = 128
BLOCK = 128
REL_BUCKETS = 32
REL_MAX_DIST = 128
D_FF = 5632
CONV_WIDTH = 3
EPS = 1e-6
NEG = -1e30
LOG2E = 1.4426950408889634

VMEM_LIMIT_BYTES = 56 * 1024 * 1024
LANES = 128
BF16_ROWS = 16
MLA_TILE = 512

Z_QS = 0
Z_GA = 2048
Z_GB = 4096
Z_CQ = 6144
Z_KS = 6912
Z_CKV = 7168
Z_VS = 7680
Z_KR = 7936
Z_KRR = 8064
Z_COLS = 8192

SH1, SC1, GT1, SH2, SC2, GT2 = range(6)


def _cparams(sem, flags=None):
    return pltpu.CompilerParams(dimension_semantics=sem, vmem_limit_bytes=VMEM_LIMIT_BYTES,
                                flags=flags)


def _rms(x, g):
    ms = jnp.mean(x * x, axis=-1, keepdims=True)
    return x * lax.rsqrt(ms + EPS) * g


def _ada_kernel(c_ref, w_ref, b_ref, o_ref):
    c = c_ref[...]
    ca = (c * jax.nn.sigmoid(c)).astype(BF16)
    o_ref[...] = jnp.dot(ca, w_ref[...].astype(BF16), preferred_element_type=F32) + b_ref[...]


def _ada(c8, w_ada, b_ada):
    n = w_ada.shape[1]
    tn = 1024
    return pl.pallas_call(
        _ada_kernel,
        grid=(n // tn,),
        in_specs=[
            pl.BlockSpec((8, D_MODEL), lambda j: (0, 0)),
            pl.BlockSpec((D_MODEL, tn), lambda j: (0, j)),
            pl.BlockSpec((1, tn), lambda j: (0, j)),
        ],
        out_specs=pl.BlockSpec((8, tn), lambda j: (0, j)),
        out_shape=jax.ShapeDtypeStruct((8, n), F32),
        compiler_params=_cparams(("arbitrary",)),
        name="ada",
    )(c8, w_ada, b_ada)


def _fill_normed(x_ref, g, sc, sh, h_ref, row0, nrows, chunk):
    def body(i, carry):
        r = pl.multiple_of(i * chunk, chunk)
        y = _rms(x_ref[pl.ds(r, chunk), :], g)
        h_ref[pl.ds(row0 + r, chunk), :] = (y * (1.0 + sc) + sh).astype(BF16)
        return carry

    lax.fori_loop(0, nrows // chunk, body, 0)


def _inproj_kernel(x_ref, g_ref, mod_ref, w_ref, o_ref, h_ref, *, tm):
    @pl.when(pl.program_id(1) == 0)
    def _():
        _fill_normed(x_ref, g_ref[...], mod_ref[SC1:SC1 + 1, :], mod_ref[SH1:SH1 + 1, :],
                     h_ref, 0, tm, 128)

    o_ref[...] = jnp.dot(h_ref[...], w_ref[...], preferred_element_type=F32).astype(o_ref.dtype)


def _inproj(x2, g, mod, w, seq):
    t, d = x2.shape
    n = w.shape[1]
    tm, tn = 1024, 1024
    tpb = seq // tm
    return pl.pallas_call(
        functools.partial(_inproj_kernel, tm=tm),
        grid=(t // tm, n // tn),
        in_specs=[
            pl.BlockSpec((tm, d), lambda i, j: (i, 0)),
            pl.BlockSpec((1, d), lambda i, j: (0, 0)),
            pl.BlockSpec((None, 6, d), lambda i, j: (i // tpb, 0, 0)),
            pl.BlockSpec((d, tn), lambda i, j: (0, j)),
        ],
        out_specs=pl.BlockSpec((tm, tn), lambda i, j: (i, j)),
        out_shape=jax.ShapeDtypeStruct((t, n), BF16),
        scratch_shapes=[pltpu.VMEM((tm, d), BF16)],
        compiler_params=_cparams(("parallel", "arbitrary")),
        name="inproj",
    )(x2, g, mod, w)


def _rope_chunks(a, b, cos, sin):
    n = a.shape[1] // LANES
    out = [a[:, LANES * i:LANES * (i + 1)] * cos + b[:, LANES * i:LANES * (i + 1)] * sin
           for i in range(n)]
    return out[0] if n == 1 else jnp.concatenate(out, axis=1)


def _qproj_kernel(cq_ref, g_ref, wn_ref, wr_ref, wrr_ref, cos_ref, sin_ref, qn_ref, qr_ref):
    cqn = _rms(cq_ref[...].astype(F32), g_ref[...]).astype(BF16)
    qn_ref[...] = jnp.dot(cqn, wn_ref[...], preferred_element_type=F32).astype(BF16)
    a = jnp.dot(cqn, wr_ref[...], preferred_element_type=F32)
    b = jnp.dot(cqn, wrr_ref[...], preferred_element_type=F32)
    qr_ref[...] = _rope_chunks(a, b, cos_ref[...], sin_ref[...]).astype(BF16)


def _qproj(z, g, wn, wr, wrr, cos2, sin2, seq):
    t = z.shape[0]
    tm = 512
    spb = seq // tm
    hn = MLA_HEADS * MLA_NOPE
    hr = MLA_HEADS * MLA_ROPE
    return pl.pallas_call(
        _qproj_kernel,
        grid=(t // tm,),
        in_specs=[
            pl.BlockSpec((tm, MLA_Q_RANK), lambda i: (i, Z_CQ // MLA_Q_RANK)),
            pl.BlockSpec((1, MLA_Q_RANK), lambda i: (0, 0)),
            pl.BlockSpec((MLA_Q_RANK, hn), lambda i: (0, 0)),
            pl.BlockSpec((MLA_Q_RANK, hr), lambda i: (0, 0)),
            pl.BlockSpec((MLA_Q_RANK, hr), lambda i: (0, 0)),
            pl.BlockSpec((tm, LANES), lambda i: (i % spb, 0)),
            pl.BlockSpec((tm, LANES), lambda i: (i % spb, 0)),
        ],
        out_specs=[
            pl.BlockSpec((tm, hn), lambda i: (i, 0)),
            pl.BlockSpec((tm, hr), lambda i: (i, 0)),
        ],
        out_shape=[jax.ShapeDtypeStruct((t, hn), BF16), jax.ShapeDtypeStruct((t, hr), BF16)],
        compiler_params=_cparams(("parallel",)),
        name="qproj",
    )(z, g, wn, wr, wrr, cos2, sin2)


def _kvproj_kernel(ckv_ref, g_ref, wk_ref, wvt_ref, kr_ref, krr_ref, cos_ref, sin_ref,
                   kn_ref, vt_ref, kro_ref):
    ckvn = _rms(ckv_ref[...].astype(F32), g_ref[...]).astype(BF16)
    kn_ref[...] = jnp.dot(ckvn, wk_ref[...], preferred_element_type=F32).astype(BF16)
    vt = lax.dot_general(wvt_ref[...], ckvn, (((1,), (1,)), ((), ())), preferred_element_type=F32)
    vt_ref[...] = vt.reshape(vt_ref.shape).astype(BF16)
    kro_ref[...] = (kr_ref[...].astype(F32) * cos_ref[...]
                    + krr_ref[...].astype(F32) * sin_ref[...]).astype(BF16)


def _kvproj(z, g, wk, wvt, cos2, sin2, batch, seq):
    t = z.shape[0]
    tm = MLA_TILE
    spb = seq // tm
    hn = MLA_HEADS * MLA_NOPE
    hv = MLA_HEADS * MLA_V
    return pl.pallas_call(
        _kvproj_kernel,
        grid=(t // tm,),
        in_specs=[
            pl.BlockSpec((tm, MLA_KV_RANK), lambda i: (i, Z_CKV // MLA_KV_RANK)),
            pl.BlockSpec((1, MLA_KV_RANK), lambda i: (0, 0)),
            pl.BlockSpec((MLA_KV_RANK, hn), lambda i: (0, 0)),
            pl.BlockSpec((hv, MLA_KV_RANK), lambda i: (0, 0)),
            pl.BlockSpec((tm, LANES), lambda i: (i, Z_KR // LANES)),
            pl.BlockSpec((tm, LANES), lambda i: (i, Z_KRR // LANES)),
            pl.BlockSpec((tm, LANES), lambda i: (i % spb, 0)),
            pl.BlockSpec((tm, LANES), lambda i: (i % spb, 0)),
        ],
        out_specs=[
            pl.BlockSpec((tm, hn), lambda i: (i, 0)),
            pl.BlockSpec((None, MLA_HEADS, None, MLA_V, tm), lambda i: (i // spb, 0, i % spb, 0, 0)),
            pl.BlockSpec((tm, LANES), lambda i: (i, 0)),
        ],
        out_shape=[jax.ShapeDtypeStruct((t, hn), BF16),
                   jax.ShapeDtypeStruct((batch, MLA_HEADS, spb, MLA_V, tm), BF16),
                   jax.ShapeDtypeStruct((t, LANES), BF16)],
        compiler_params=_cparams(("parallel",)),
        name="kvproj",
    )(z, g, wk, wvt, z, z, cos2, sin2)


def _mla_kernel(qn_ref, qr_ref, kn_ref, kr_ref, vt_ref, o_ref, q_ref, s_ref, p_ref, acc_ref,
                *, tile, nq):
    head = pl.program_id(1)
    lo = (head % 2) * MLA_ROPE

    def scores(q, k):
        k0 = k * tile if isinstance(k, int) else pl.multiple_of(k * tile, tile)
        kc = jnp.concatenate([kn_ref[pl.ds(k0, tile), :], kr_ref[pl.ds(k0, tile), :]], axis=1)
        return lax.dot_general(kc, q, (((1,), (1,)), ((), ())), preferred_element_type=F32)

    def softmax(s, m, l):
        m_new = jnp.maximum(m, jnp.max(s, axis=0, keepdims=True))
        alpha = jnp.exp2(m - m_new)
        p = jnp.exp2(s - m_new)
        return p.astype(BF16), alpha, m_new, alpha * l + jnp.sum(p, axis=0, keepdims=True)

    def values(p, k):
        return jnp.dot(vt_ref[k], p, preferred_element_type=F32)

    for qi in range(nq):
        rows = slice(qi * tile, (qi + 1) * tile)
        qr = qr_ref[rows, :]
        lane = lax.broadcasted_iota(jnp.int32, qr.shape, 1)
        keep = (lane >= lo) & (lane < lo + MLA_ROPE)
        q_ref[...] = jnp.concatenate(
            [qn_ref[rows, :], jnp.where(keep, qr, jnp.zeros_like(qr))], axis=1)

        s_ref[0] = scores(q_ref[...], 0)
        m = jnp.full((1, tile), NEG, F32)
        l = jnp.zeros((1, tile), F32)
        for k in range(qi + 1):
            if k < qi:
                s_ref[(k + 1) % 3] = scores(q_ref[...], k + 1)
            s = s_ref[k % 3]
            if k == qi:
                key = lax.broadcasted_iota(jnp.int32, s.shape, 0)
                qry = lax.broadcasted_iota(jnp.int32, s.shape, 1)
                s = jnp.where(key <= qry, s, NEG)
            p, a, m, l = softmax(s, m, l)
            p_ref[k % 2] = p
            pv = values(p_ref[k % 2], k)
            acc_ref[...] = pv if k == 0 else a * acc_ref[...] + pv
        o_ref[rows, :] = (acc_ref[...] / l).T.astype(o_ref.dtype)


def _mla(qn, qr, kn, kr2, vt, batch, seq):
    tile = MLA_TILE
    nq = seq // tile
    t = batch * seq
    tok = lambda b, h: (b, h)
    return pl.pallas_call(
        functools.partial(_mla_kernel, tile=tile, nq=nq),
        grid=(batch, MLA_HEADS),
        in_specs=[
            pl.BlockSpec((seq, LANES), tok),
            pl.BlockSpec((seq, LANES), lambda b, h: (b, h // 2)),
            pl.BlockSpec((seq, LANES), tok),
            pl.BlockSpec((seq, LANES), lambda b, h: (b, 0)),
            pl.BlockSpec((None, None, nq, MLA_V, tile), lambda b, h: (b, h, 0, 0, 0)),
        ],
        out_specs=pl.BlockSpec((seq, LANES), tok),
        out_shape=jax.ShapeDtypeStruct((t, MLA_HEADS * MLA_V), BF16),
        scratch_shapes=[
            pltpu.VMEM((tile, 2 * LANES), BF16),
            pltpu.VMEM((3, tile, tile), F32),
            pltpu.VMEM((2, tile, tile), BF16),
            pltpu.VMEM((MLA_V, tile), F32),
        ],
        compiler_params=_cparams(("parallel", "arbitrary")),
        name="mla",
    )(qn, qr, kn, kr2, vt)


def _t5_bucket(dist):
    max_exact = REL_BUCKETS // 2
    n = jnp.maximum(dist, 0)
    large = max_exact + (jnp.log(jnp.maximum(n, 1).astype(F32) / max_exact)
                         / math.log(REL_MAX_DIST / max_exact)
                         * (REL_BUCKETS - max_exact)).astype(jnp.int32)
    large = jnp.minimum(large, REL_BUCKETS - 1)
    return jnp.where(n < max_exact, n, large)


def _t5bias_kernel(rb_ref, bucket_ref, o_ref):
    head = pl.program_id(0)
    bucket = bucket_ref[...]
    acc = jnp.zeros(bucket.shape, F32)
    for b in range(REL_BUCKETS):
        acc = jnp.where(bucket == b, rb_ref[b, head], acc)
    o_ref[...] = acc


def _t5bias(rel_bias):
    a = jnp.arange(BLOCK)
    bidx = jnp.arange(2 * BLOCK)
    bucket = _t5_bucket(BLOCK + a[:, None] - bidx[None, :]).astype(jnp.int32)
    return pl.pallas_call(
        _t5bias_kernel,
        grid=(SWA_HEADS,),
        in_specs=[
            pl.BlockSpec(memory_space=pltpu.SMEM),
            pl.BlockSpec((BLOCK, 2 * BLOCK), lambda h: (0, 0)),
        ],
        out_specs=pl.BlockSpec((None, BLOCK, 2 * BLOCK), lambda h: (h, 0, 0)),
        out_shape=jax.ShapeDtypeStruct((SWA_HEADS, BLOCK, 2 * BLOCK), F32),
        compiler_params=_cparams(("arbitrary",)),
        name="t5bias",
    )(rel_bias, bucket)


def _swa_kernel(sink_ref, q_ref, kp_ref, kc_ref, vp_ref, vc_ref, bias_ref, o_ref):
    n = pl.program_id(1)
    q = q_ref[...]
    k = jnp.concatenate([kp_ref[...], kc_ref[...]], axis=0)
    v = jnp.concatenate([vp_ref[...], vc_ref[...]], axis=0)
    row = lax.broadcasted_iota(jnp.int32, (BLOCK, 2 * BLOCK), 0)
    col = lax.broadcasted_iota(jnp.int32, (BLOCK, 2 * BLOCK), 1)
    dist = BLOCK + row - col
    first_col = jnp.where(n > 0, 0, BLOCK)
    mask = (dist >= 0) & (dist < WINDOW) & (col >= first_col)
    outs = []
    for hq in range(SWA_HEADS):
        kvh = hq // SWA_GROUP
        lo = SWA_HEAD_DIM * kvh
        k_h = k[:, lo:lo + SWA_HEAD_DIM]
        v_h = v[:, lo:lo + SWA_HEAD_DIM]
        q_h = q[:, SWA_HEAD_DIM * hq:SWA_HEAD_DIM * (hq + 1)]
        s = lax.dot_general(q_h, k_h, (((1,), (1,)), ((), ())), preferred_element_type=F32)
        s = jnp.where(mask, s + bias_ref[hq], NEG)
        sink = sink_ref[hq]
        m = jnp.maximum(jnp.max(s, axis=-1, keepdims=True), sink)
        e = jnp.exp(s - m)
        denom = jnp.sum(e, axis=-1, keepdims=True) + jnp.exp(sink - m)
        o = jnp.dot(e.astype(BF16), v_h, preferred_element_type=F32)
        outs.append((o / denom).astype(BF16))
    o_ref[...] = jnp.concatenate(outs, axis=1)


def _swa(z, bias, sinks, batch, seq):
    nb = seq // BLOCK
    t = batch * seq
    qcols = SWA_HEADS * SWA_HEAD_DIM

    def cur(col):
        return pl.BlockSpec((BLOCK, SWA_KV), lambda b, n: (b * nb + n, col))

    def prev(col):
        return pl.BlockSpec((BLOCK, SWA_KV), lambda b, n: (b * nb + jnp.maximum(n - 1, 0), col))

    return pl.pallas_call(
        _swa_kernel,
        grid=(batch, nb),
        in_specs=[
            pl.BlockSpec(memory_space=pltpu.SMEM),
            pl.BlockSpec((BLOCK, qcols), lambda b, n: (b * nb + n, Z_QS // qcols)),
            prev(Z_KS // SWA_KV), cur(Z_KS // SWA_KV),
            prev(Z_VS // SWA_KV), cur(Z_VS // SWA_KV),
            pl.BlockSpec((SWA_HEADS, BLOCK, 2 * BLOCK), lambda b, n: (0, 0, 0)),
        ],
        out_specs=pl.BlockSpec((BLOCK, qcols), lambda b, n: (b * nb + n, 0)),
        out_shape=jax.ShapeDtypeStruct((t, qcols), BF16),
        compiler_params=_cparams(("parallel", "arbitrary")),
        name="swa",
    )(sinks, z, z, z, z, z, bias)


def _mix_kernel(oa_ref, ob_ref, ga_ref, gb_ref, w_ref, x_ref, g_ref, mod_ref, o_ref, *, tm, sub):
    for r in range(tm // sub):
        rows = slice(r * sub, (r + 1) * sub)
        ga = jax.nn.sigmoid(ga_ref[rows, :].astype(F32))
        gb = jax.nn.sigmoid(gb_ref[rows, :].astype(F32))
        u = ga * oa_ref[rows, :].astype(F32) + gb * ob_ref[rows, :].astype(F32)
        mix = jnp.dot(u.astype(BF16), w_ref[...], preferred_element_type=F32)
        o_ref[rows, :] = x_ref[rows, :] + mod_ref[GT1:GT1 + 1, :] * _rms(mix, g_ref[...])


def _mix(oa, ob, z, w, x2, g, mod, seq):
    t, d = x2.shape
    tm, sub = 512, 256
    tpb = seq // tm
    row = lambda i: (i, 0)
    return pl.pallas_call(
        functools.partial(_mix_kernel, tm=tm, sub=sub),
        grid=(t // tm,),
        in_specs=[
            pl.BlockSpec((tm, d), row),
            pl.BlockSpec((tm, d), row),
            pl.BlockSpec((tm, d), lambda i: (i, Z_GA // d)),
            pl.BlockSpec((tm, d), lambda i: (i, Z_GB // d)),
            pl.BlockSpec((d, d), lambda i: (0, 0), pipeline_mode=pl.Buffered(1)),
            pl.BlockSpec((tm, d), row),
            pl.BlockSpec((1, d), lambda i: (0, 0)),
            pl.BlockSpec((None, 6, d), lambda i: (i // tpb, 0, 0)),
        ],
        out_specs=pl.BlockSpec((tm, d), row),
        out_shape=jax.ShapeDtypeStruct((t, d), F32),
        compiler_params=_cparams(("parallel",)),
        name="mix",
    )(oa, ob, z, z, w, x2, g, mod)


def _ffn_up_kernel(x_ref, halo_ref, g_ref, mod_ref, wg_ref, wv_ref, cwg_ref, cwv_ref,
                   cbg_ref, cbv_ref, o_ref, h_ref, *, tm, tn, sub, rb, tpb):
    i = pl.program_id(0)

    @pl.when(pl.program_id(1) == 0)
    def _():
        g = g_ref[...]
        sc = mod_ref[SC2:SC2 + 1, :]
        sh = mod_ref[SH2:SH2 + 1, :]
        halo = (_rms(halo_ref[...], g) * (1.0 + sc) + sh).astype(BF16)
        h_ref[0:BF16_ROWS, :] = jnp.where(i % tpb == 0, jnp.zeros_like(halo), halo)
        _fill_normed(x_ref, g, sc, sh, h_ref, BF16_ROWS, tm, 128)

    def conv(hblk, tail, w_ref, cw_ref, cb_ref, cols):
        u = jnp.dot(hblk, w_ref[:, cols], preferred_element_type=F32)
        drop = BF16_ROWS
        if tail is not None:
            u = jnp.concatenate([tail, u], axis=0)
            drop = tail.shape[0]
        y = cw_ref[1:2, cols] * u + pltpu.roll(cw_ref[0:1, cols] * u, 1, 0)
        y = cw_ref[2:3, cols] * u + pltpu.roll(y, 1, 0)
        return cb_ref[:, cols] + y[drop:, :], u[u.shape[0] - 8:, :]

    for c in range(tn // sub):
        cols = slice(c * sub, (c + 1) * sub)
        gtail = vtail = None
        for r in range(tm // rb):
            lo = 0 if r == 0 else BF16_ROWS + r * rb
            hblk = h_ref[lo:BF16_ROWS + (r + 1) * rb, :]
            gate, gtail = conv(hblk, gtail, wg_ref, cwg_ref, cbg_ref, cols)
            val, vtail = conv(hblk, vtail, wv_ref, cwv_ref, cbv_ref, cols)
            o_ref[r * rb:(r + 1) * rb, cols] = (
                gate * jax.nn.sigmoid(gate) * val).astype(o_ref.dtype)


def _ffn_up(x1, g, mod, w_up, conv_w, conv_b, seq):
    t, d = x1.shape
    tm, tn, sub, rb = 1024, 512, 256, 256
    tpb = seq // tm
    nj = D_FF // tn
    hb = tm // BF16_ROWS
    return pl.pallas_call(
        functools.partial(_ffn_up_kernel, tm=tm, tn=tn, sub=sub, rb=rb, tpb=tpb),
        grid=(t // tm, nj),
        in_specs=[
            pl.BlockSpec((tm, d), lambda i, j: (i, 0)),
            pl.BlockSpec((BF16_ROWS, d), lambda i, j: (jnp.maximum(i * hb - 1, 0), 0)),
            pl.BlockSpec((1, d), lambda i, j: (0, 0)),
            pl.BlockSpec((None, 6, d), lambda i, j: (i // tpb, 0, 0)),
            pl.BlockSpec((d, tn), lambda i, j: (0, j)),
            pl.BlockSpec((d, tn), lambda i, j: (0, j + nj)),
            pl.BlockSpec((CONV_WIDTH, tn), lambda i, j: (0, j)),
            pl.BlockSpec((CONV_WIDTH, tn), lambda i, j: (0, j + nj)),
            pl.BlockSpec((1, tn), lambda i, j: (0, j)),
            pl.BlockSpec((1, tn), lambda i, j: (0, j + nj)),
        ],
        out_specs=pl.BlockSpec((tm, tn), lambda i, j: (i, j)),
        out_shape=jax.ShapeDtypeStruct((t, D_FF), BF16),
        scratch_shapes=[pltpu.VMEM((tm + BF16_ROWS, d), BF16)],
        compiler_params=_cparams(("parallel", "arbitrary")),
        name="ffn_up",
    )(x1, x1, g, mod, w_up, w_up, conv_w, conv_w, conv_b, conv_b)


def _ffn_down_kernel(a_ref, w_ref, x_ref, g_ref, mod_ref, o_ref):
    y = jnp.dot(a_ref[...], w_ref[...], preferred_element_type=F32)
    o_ref[...] = x_ref[...] + mod_ref[GT2:GT2 + 1, :] * _rms(y, g_ref[...])


def _ffn_down(a, w, x1, g, mod, seq):
    t, d = x1.shape
    tm = 256
    tpb = seq // tm
    return pl.pallas_call(
        _ffn_down_kernel,
        grid=(t // tm,),
        in_specs=[
            pl.BlockSpec((tm, D_FF), lambda i: (i, 0)),
            pl.BlockSpec((D_FF, d), lambda i: (0, 0), pipeline_mode=pl.Buffered(1)),
            pl.BlockSpec((tm, d), lambda i: (i, 0)),
            pl.BlockSpec((1, d), lambda i: (0, 0)),
            pl.BlockSpec((None, 6, d), lambda i: (i // tpb, 0, 0)),
        ],
        out_specs=pl.BlockSpec((tm, d), lambda i: (i, 0)),
        out_shape=jax.ShapeDtypeStruct((t, d), F32),
        compiler_params=_cparams(("parallel",)),
        name="ffn_down",
    )(a, w, x1, g, mod)


def _rot_cols(w):
    half = MLA_ROPE // 2
    w = w.reshape(w.shape[0], -1, MLA_ROPE)
    return jnp.concatenate([-w[..., half:], w[..., :half]], axis=-1).reshape(w.shape[0], -1)


def _prep_w_in(w_in):
    o = 0
    cq = w_in[:, o:o + MLA_Q_RANK]; o += MLA_Q_RANK
    ckv = w_in[:, o:o + MLA_KV_RANK]; o += MLA_KV_RANK
    kr = w_in[:, o:o + MLA_ROPE]; o += MLA_ROPE
    qs = w_in[:, o:o + D_MODEL]; o += D_MODEL
    ks = w_in[:, o:o + SWA_KV]; o += SWA_KV
    vs = w_in[:, o:o + SWA_KV]; o += SWA_KV
    gates = w_in[:, o:o + 2 * D_MODEL]
    krr = _rot_cols(kr)
    w = jnp.concatenate([qs * SWA_HEAD_DIM ** -0.5, gates, cq, ks, ckv, vs, kr, kr, krr, krr],
                        axis=1)
    assert w.shape[1] == Z_COLS
    return w.astype(BF16)


def _prep_w_uq(w_uq):
    w = (w_uq * (MLA_QK ** -0.5 * LOG2E)).reshape(MLA_Q_RANK, MLA_HEADS, MLA_QK)
    wn = w[..., :MLA_NOPE].reshape(MLA_Q_RANK, -1)
    wr = w[..., MLA_NOPE:].reshape(MLA_Q_RANK, -1)
    return wn.astype(BF16), wr.astype(BF16), _rot_cols(wr).astype(BF16)


def _prep_w_ukv(w_ukv):
    w = w_ukv.reshape(MLA_KV_RANK, MLA_HEADS, MLA_NOPE + MLA_V)
    wk = w[..., :MLA_NOPE].reshape(MLA_KV_RANK, -1)
    wvt = w[..., MLA_NOPE:].reshape(MLA_KV_RANK, -1).T
    return wk.astype(BF16), wvt.astype(BF16)


def _rope_tables2(seq):
    pos = jnp.arange(seq, dtype=F32)
    inv = ROPE_THETA ** (-jnp.arange(0, MLA_ROPE, 2, dtype=F32) / MLA_ROPE)
    ang = pos[:, None] * inv[None, :]
    ang = jnp.concatenate([ang, ang, ang, ang], axis=-1)
    return jnp.cos(ang), jnp.sin(ang)


def kernel(x, c, w_ada, b_ada, g_pre_mix, g_post_mix, w_in, g_q_lat, w_uq, g_kv_lat, w_ukv,
           rel_bias, sinks, w_o, g_pre_ffn, g_post_ffn, w_up, conv_w, conv_b, w_down):
    batch, seq, d = x.shape
    depth = w_ada.shape[0]
    xt = x.reshape(batch * seq, d)
    c8 = jnp.pad(c, ((0, 8 - batch), (0, 0)))
    cos2, sin2 = _rope_tables2(seq)
    bias = _t5bias(rel_bias)
    for l in range(depth):
        mod = _ada(c8, w_ada[l], b_ada[l][None, :])[:batch].reshape(batch, 6, d)
        z = _inproj(xt, g_pre_mix[l][None, :], mod, _prep_w_in(w_in[l]), seq)
        wn, wr, wrr = _prep_w_uq(w_uq[l])
        qn, qr = _qproj(z, g_q_lat[l][None, :], wn, wr, wrr, cos2, sin2, seq)
        wk, wvt = _prep_w_ukv(w_ukv[l])
        kn, vt, kr2 = _kvproj(z, g_kv_lat[l][None, :], wk, wvt, cos2, sin2, batch, seq)
        o_a = _mla(qn, qr, kn, kr2, vt, batch, seq)
        o_b = _swa(z, bias, sinks[l], batch, seq)
        x1 = _mix(o_a, o_b, z, w_o[l].astype(BF16), xt, g_post_mix[l][None, :], mod, seq)
        a = _ffn_up(x1, g_pre_ffn[l][None, :], mod, w_up[l].astype(BF16), conv_w[l],
                    conv_b[l][None, :], seq)
        xt = _ffn_down(a, w_down[l].astype(BF16), x1, g_post_ffn[l][None, :], mod, seq)
    return xt.reshape(batch, seq, d)
```

```python
import functools
import math

import jax
import jax.numpy as jnp
from jax import lax
from jax.experimental import pallas as pl
from jax.experimental.pallas import tpu as pltpu

F32 = jnp.float32
BF16 = jnp.bfloat16

D_MODEL = 2048
MLA_NOPE = 128
MLA_ROPE = 64
MLA_V = 128
MLA_HEADS = D_MODEL // MLA_V
MLA_Q_RANK = 768
MLA_KV_RANK = 512
MLA_QK = MLA_NOPE + MLA_ROPE
ROPE_THETA = 10000.0
SWA_HEAD_DIM = 64
SWA_HEADS = D_MODEL // SWA_HEAD_DIM
SWA_KV_HEADS = 4
SWA_GROUP = SWA_HEADS // SWA_KV_HEADS
SWA_KV = SWA_KV_HEADS * SWA_HEAD_DIM
WINDOW = 128
BLOCK = 128
REL_BUCKETS = 32
REL_MAX_DIST = 128
D_FF = 5632
CONV_WIDTH = 3
EPS = 1e-6
NEG = -1e30
LOG2E = 1.4426950408889634

VMEM_LIMIT_BYTES = 56 * 1024 * 1024
LANES = 128
BF16_ROWS = 16
MLA_TILE = 512

Z_QS = 0
Z_GA = 2048
Z_GB = 4096
Z_CQ = 6144
Z_KS = 6912
Z_CKV = 7168
Z_VS = 7680
Z_KR = 7936
Z_KRR = 8064
Z_COLS = 8192

SH1, SC1, GT1, SH2, SC2, GT2 = range(6)


def _cparams(sem, flags=None):
    return pltpu.CompilerParams(dimension_semantics=sem, vmem_limit_bytes=VMEM_LIMIT_BYTES,
                                flags=flags)


def _rms(x, g):
    ms = jnp.mean(x * x, axis=-1, keepdims=True)
    return x * lax.rsqrt(ms + EPS) * g


def _ada_kernel(c_ref, w_ref, b_ref, o_ref):
    c = c_ref[...]
    ca = (c * jax.nn.sigmoid(c)).astype(BF16)
    o_ref[...] = jnp.dot(ca, w_ref[...].astype(BF16), preferred_element_type=F32) + b_ref[...]


def _ada(c8, w_ada, b_ada):
    n = w_ada.shape[1]
    tn = 1024
    return pl.pallas_call(
        _ada_kernel,
        grid=(n // tn,),
        in_specs=[
            pl.BlockSpec((8, D_MODEL), lambda j: (0, 0)),
            pl.BlockSpec((D_MODEL, tn), lambda j: (0, j)),
            pl.BlockSpec((1, tn), lambda j: (0, j)),
        ],
        out_specs=pl.BlockSpec((8, tn), lambda j: (0, j)),
        out_shape=jax.ShapeDtypeStruct((8, n), F32),
        compiler_params=_cparams(("arbitrary",)),
        name="ada",
    )(c8, w_ada, b_ada)


def _fill_normed(x_ref, g, sc, sh, h_ref, row0, nrows, chunk):
    def body(i, carry):
        r = pl.multiple_of(i * chunk, chunk)
        y = _rms(x_ref[pl.ds(r, chunk), :], g)
        h_ref[pl.ds(row0 + r, chunk), :] = (y * (1.0 + sc) + sh).astype(BF16)
        return carry

    lax.fori_loop(0, nrows // chunk, body, 0)


def _inproj_kernel(x_ref, g_ref, mod_ref, w_ref, o_ref, h_ref, *, tm):
    @pl.when(pl.program_id(1) == 0)
    def _():
        _fill_normed(x_ref, g_ref[...], mod_ref[SC1:SC1 + 1, :], mod_ref[SH1:SH1 + 1, :],
                     h_ref, 0, tm, 128)

    o_ref[...] = jnp.dot(h_ref[...], w_ref[...], preferred_element_type=F32).astype(o_ref.dtype)


def _inproj(x2, g, mod, w, seq):
    t, d = x2.shape
    n = w.shape[1]
    tm, tn = 1024, 1024
    tpb = seq // tm
    return pl.pallas_call(
        functools.partial(_inproj_kernel, tm=tm),
        grid=(t // tm, n // tn),
        in_specs=[
            pl.BlockSpec((tm, d), lambda i, j: (i, 0)),
            pl.BlockSpec((1, d), lambda i, j: (0, 0)),
            pl.BlockSpec((None, 6, d), lambda i, j: (i // tpb, 0, 0)),
            pl.BlockSpec((d, tn), lambda i, j: (0, j)),
        ],
        out_specs=pl.BlockSpec((tm, tn), lambda i, j: (i, j)),
        out_shape=jax.ShapeDtypeStruct((t, n), BF16),
        scratch_shapes=[pltpu.VMEM((tm, d), BF16)],
        compiler_params=_cparams(("parallel", "arbitrary")),
        name="inproj",
    )(x2, g, mod, w)


def _rope_chunks(a, b, cos, sin):
    n = a.shape[1] // LANES
    out = [a[:, LANES * i:LANES * (i + 1)] * cos + b[:, LANES * i:LANES * (i + 1)] * sin
           for i in range(n)]
    return out[0] if n == 1 else jnp.concatenate(out, axis=1)


def _qproj_kernel(cq_ref, g_ref, wn_ref, wr_ref, wrr_ref, cos_ref, sin_ref, qn_ref, qr_ref):
    cqn = _rms(cq_ref[...].astype(F32), g_ref[...]).astype(BF16)
    qn_ref[...] = jnp.dot(cqn, wn_ref[...], preferred_element_type=F32).astype(BF16)
    a = jnp.dot(cqn, wr_ref[...], preferred_element_type=F32)
    b = jnp.dot(cqn, wrr_ref[...], preferred_element_type=F32)
    qr_ref[...] = _rope_chunks(a, b, cos_ref[...], sin_ref[...]).astype(BF16)


def _qproj(z, g, wn, wr, wrr, cos2, sin2, seq):
    t = z.shape[0]
    tm = 512
    spb = seq // tm
    hn = MLA_HEADS * MLA_NOPE
    hr = MLA_HEADS * MLA_ROPE
    return pl.pallas_call(
        _qproj_kernel,
        grid=(t // tm,),
        in_specs=[
            pl.BlockSpec((tm, MLA_Q_RANK), lambda i: (i, Z_CQ // MLA_Q_RANK)),
            pl.BlockSpec((1, MLA_Q_RANK), lambda i: (0, 0)),
            pl.BlockSpec((MLA_Q_RANK, hn), lambda i: (0, 0)),
            pl.BlockSpec((MLA_Q_RANK, hr), lambda i: (0, 0)),
            pl.BlockSpec((MLA_Q_RANK, hr), lambda i: (0, 0)),
            pl.BlockSpec((tm, LANES), lambda i: (i % spb, 0)),
            pl.BlockSpec((tm, LANES), lambda i: (i % spb, 0)),
        ],
        out_specs=[
            pl.BlockSpec((tm, hn), lambda i: (i, 0)),
            pl.BlockSpec((tm, hr), lambda i: (i, 0)),
        ],
        out_shape=[jax.ShapeDtypeStruct((t, hn), BF16), jax.ShapeDtypeStruct((t, hr), BF16)],
        compiler_params=_cparams(("parallel",)),
        name="qproj",
    )(z, g, wn, wr, wrr, cos2, sin2)


def _kvproj_kernel(ckv_ref, g_ref, wk_ref, wvt_ref, kr_ref, krr_ref, cos_ref, sin_ref,
                   kn_ref, vt_ref, kro_ref):
    ckvn = _rms(ckv_ref[...].astype(F32), g_ref[...]).astype(BF16)
    kn_ref[...] = jnp.dot(ckvn, wk_ref[...], preferred_element_type=F32).astype(BF16)
    vt = lax.dot_general(wvt_ref[...], ckvn, (((1,), (1,)), ((), ())), preferred_element_type=F32)
    vt_ref[...] = vt.reshape(vt_ref.shape).astype(BF16)
    kro_ref[...] = (kr_ref[...].astype(F32) * cos_ref[...]
                    + krr_ref[...].astype(F32) * sin_ref[...]).astype(BF16)


def _kvproj(z, g, wk, wvt, cos2, sin2, batch, seq):
    t = z.shape[0]
    tm = MLA_TILE
    spb = seq // tm
    hn = MLA_HEADS * MLA_NOPE
    hv = MLA_HEADS * MLA_V
    return pl.pallas_call(
        _kvproj_kernel,
        grid=(t // tm,),
        in_specs=[
            pl.BlockSpec((tm, MLA_KV_RANK), lambda i: (i, Z_CKV // MLA_KV_RANK)),
            pl.BlockSpec((1, MLA_KV_RANK), lambda i: (0, 0)),
            pl.BlockSpec((MLA_KV_RANK, hn), lambda i: (0, 0)),
            pl.BlockSpec((hv, MLA_KV_RANK), lambda i: (0, 0)),
            pl.BlockSpec((tm, LANES), lambda i: (i, Z_KR // LANES)),
            pl.BlockSpec((tm, LANES), lambda i: (i, Z_KRR // LANES)),
            pl.BlockSpec((tm, LANES), lambda i: (i % spb, 0)),
            pl.BlockSpec((tm, LANES), lambda i: (i % spb, 0)),
        ],
        out_specs=[
            pl.BlockSpec((tm, hn), lambda i: (i, 0)),
            pl.BlockSpec((None, MLA_HEADS, None, MLA_V, tm), lambda i: (i // spb, 0, i % spb, 0, 0)),
            pl.BlockSpec((tm, LANES), lambda i: (i, 0)),
        ],
        out_shape=[jax.ShapeDtypeStruct((t, hn), BF16),
                   jax.ShapeDtypeStruct((batch, MLA_HEADS, spb, MLA_V, tm), BF16),
                   jax.ShapeDtypeStruct((t, LANES), BF16)],
        compiler_params=_cparams(("parallel",)),
        name="kvproj",
    )(z, g, wk, wvt, z, z, cos2, sin2)


def _mla_kernel(qn_ref, qr_ref, kn_ref, kr_ref, vt_ref, o_ref, q_ref, s_ref, p_ref, acc_ref,
                *, tile, nq):
    head = pl.program_id(1)
    lo = (head % 2) * MLA_ROPE

    def scores(q, k):
        k0 = k * tile if isinstance(k, int) else pl.multiple_of(k * tile, tile)
        kc = jnp.concatenate([kn_ref[pl.ds(k0, tile), :], kr_ref[pl.ds(k0, tile), :]], axis=1)
        return lax.dot_general(kc, q, (((1,), (1,)), ((), ())), preferred_element_type=F32)

    def softmax(s, m, l):
        m_new = jnp.maximum(m, jnp.max(s, axis=0, keepdims=True))
        alpha = jnp.exp2(m - m_new)
        p = jnp.exp2(s - m_new)
        return p.astype(BF16), alpha, m_new, alpha * l + jnp.sum(p, axis=0, keepdims=True)

    def values(p, k):
        return jnp.dot(vt_ref[k], p, preferred_element_type=F32)

    for qi in range(nq):
        rows = slice(qi * tile, (qi + 1) * tile)
        qr = qr_ref[rows, :]
        lane = lax.broadcasted_iota(jnp.int32, qr.shape, 1)
        keep = (lane >= lo) & (lane < lo + MLA_ROPE)
        q_ref[...] = jnp.concatenate(
            [qn_ref[rows, :], jnp.where(keep, qr, jnp.zeros_like(qr))], axis=1)

        s_ref[0] = scores(q_ref[...], 0)
        m = jnp.full((1, tile), NEG, F32)
        l = jnp.zeros((1, tile), F32)
        for k in range(qi + 1):
            if k < qi:
                s_ref[(k + 1) % 3] = scores(q_ref[...], k + 1)
            s = s_ref[k % 3]
            if k == qi:
                key = lax.broadcasted_iota(jnp.int32, s.shape, 0)
                qry = lax.broadcasted_iota(jnp.int32, s.shape, 1)
                s = jnp.where(key <= qry, s, NEG)
            p, a, m, l = softmax(s, m, l)
            p_ref[k % 2] = p
            pv = values(p_ref[k % 2], k)
            acc_ref[...] = pv if k == 0 else a * acc_ref[...] + pv
        o_ref[rows, :] = (acc_ref[...] / l).T.astype(o_ref.dtype)


def _mla(qn, qr, kn, kr2, vt, batch, seq):
    tile = MLA_TILE
    nq = seq // tile
    t = batch * seq
    tok = lambda b, h: (b, h)
    return pl.pallas_call(
        functools.partial(_mla_kernel, tile=tile, nq=nq),
        grid=(batch, MLA_HEADS),
        in_specs=[
            pl.BlockSpec((seq, LANES), tok),
            pl.BlockSpec((seq, LANES), lambda b, h: (b, h // 2)),
            pl.BlockSpec((seq, LANES), tok),
            pl.BlockSpec((seq, LANES), lambda b, h: (b, 0)),
            pl.BlockSpec((None, None, nq, MLA_V, tile), lambda b, h: (b, h, 0, 0, 0)),
        ],
        out_specs=pl.BlockSpec((seq, LANES), tok),
        out_shape=jax.ShapeDtypeStruct((t, MLA_HEADS * MLA_V), BF16),
        scratch_shapes=[
            pltpu.VMEM((tile, 2 * LANES), BF16),
            pltpu.VMEM((3, tile, tile), F32),
            pltpu.VMEM((2, tile, tile), BF16),
            pltpu.VMEM((MLA_V, tile), F32),
        ],
        compiler_params=_cparams(("parallel", "arbitrary")),
        name="mla",
    )(qn, qr, kn, kr2, vt)


def _t5_bucket(dist):
    max_exact = REL_BUCKETS // 2
    n = jnp.maximum(dist, 0)
    large = max_exact + (jnp.log(jnp.maximum(n, 1).astype(F32) / max_exact)
                         / math.log(REL_MAX_DIST / max_exact)
                         * (REL_BUCKETS - max_exact)).astype(jnp.int32)
    large = jnp.minimum(large, REL_BUCKETS - 1)
    return jnp.where(n < max_exact, n, large)


def _t5bias_kernel(rb_ref, bucket_ref, o_ref):
    kvh = pl.program_id(0)
    bucket = bucket_ref[...]
    key = lax.broadcasted_iota(jnp.int32, bucket.shape, 0)
    qry = lax.broadcasted_iota(jnp.int32, bucket.shape, 1)
    dist = BLOCK + qry - key
    mask = (dist >= 0) & (dist < WINDOW)
    mask_first = mask & (key >= BLOCK)
    for g in range(SWA_GROUP):
        acc = jnp.zeros(bucket.shape, F32)
        for b in range(REL_BUCKETS):
            acc = jnp.where(bucket == b, rb_ref[b, kvh * SWA_GROUP + g], acc)
        cols = slice(g * BLOCK, (g + 1) * BLOCK)
        o_ref[0, :, cols] = jnp.where(mask, acc * LOG2E, NEG)
        o_ref[1, :, cols] = jnp.where(mask_first, acc * LOG2E, NEG)


def _t5bias(rel_bias):
    a = jnp.arange(BLOCK)
    bidx = jnp.arange(2 * BLOCK)
    bucket = _t5_bucket(BLOCK + a[None, :] - bidx[:, None]).astype(jnp.int32)
    return pl.pallas_call(
        _t5bias_kernel,
        grid=(SWA_KV_HEADS,),
        in_specs=[
            pl.BlockSpec(memory_space=pltpu.SMEM),
            pl.BlockSpec((2 * BLOCK, BLOCK), lambda h: (0, 0)),
        ],
        out_specs=pl.BlockSpec((2, None, 2 * BLOCK, SWA_GROUP * BLOCK), lambda h: (0, h, 0, 0)),
        out_shape=jax.ShapeDtypeStruct((2, SWA_KV_HEADS, 2 * BLOCK, SWA_GROUP * BLOCK), F32),
        compiler_params=_cparams(("arbitrary",)),
        name="t5bias",
    )(rel_bias, bucket)


def _roll_half(x):
    return pltpu.bitcast(pltpu.roll(pltpu.bitcast(x, jnp.uint32), LANES // 2, 1), x.dtype)


def _keep_lanes(x, word_mask):
    return pltpu.bitcast(pltpu.bitcast(x, jnp.uint32) & word_mask, x.dtype)


def _swa_kernel(sink_ref, q_ref, kp_ref, kc_ref, vp_ref, vc_ref, bias_ref, o_ref, *, nqb):
    seq_start = jnp.where(pl.program_id(1) == 0, 1, 0)
    k_all = jnp.concatenate([kp_ref[...], kc_ref[...]], axis=0)
    v_all = jnp.concatenate([vp_ref[...], vc_ref[...]], axis=0)
    vt_all = v_all.astype(F32).T.astype(BF16)
    low = lax.broadcasted_iota(jnp.int32, (BLOCK, LANES), 1) < SWA_HEAD_DIM
    low2 = lax.broadcasted_iota(jnp.int32, (2 * BLOCK, LANES), 1) < SWA_HEAD_DIM
    ones = jnp.ones((BF16_ROWS, 2 * BLOCK), BF16)
    lane = lax.broadcasted_iota(jnp.int32, (BLOCK // 2, LANES), 1)
    keep_low = jnp.where(lane < SWA_HEAD_DIM, jnp.uint32(0xFFFFFFFF), jnp.uint32(0))
    keep_high = ~keep_low
    for qb in range(nqb):
        first = seq_start if qb == 0 else 0
        q = q_ref[qb * BLOCK:(qb + 1) * BLOCK, :]
        k = k_all[qb * BLOCK:(qb + 2) * BLOCK, :]
        vt = vt_all[:, qb * BLOCK:(qb + 2) * BLOCK]
        outs = []
        for kvh in range(SWA_KV_HEADS):
            pair = slice(LANES * (kvh // 2), LANES * (kvh // 2 + 1))
            kb = k[:, pair]
            kr = _roll_half(kb)
            kd = jnp.where(low2, kr, kb) if kvh % 2 else jnp.where(low2, kb, kr)
            vth = vt[SWA_HEAD_DIM * kvh:SWA_HEAD_DIM * (kvh + 1), :]
            vtx = jnp.concatenate([vth, vth, ones], axis=0)
            qs, sinks = [], []
            for g in range(SWA_GROUP):
                hq = kvh * SWA_GROUP + g
                blk = q[:, LANES * (hq // 2):LANES * (hq // 2 + 1)]
                qs.append(_keep_lanes(blk, keep_low if hq % 2 == 0 else keep_high))
                sinks.append(jnp.full((1, BLOCK), sink_ref[hq] * LOG2E, F32))
            qg = jnp.concatenate(qs, axis=0)
            sink = jnp.concatenate(sinks, axis=1)
            s = lax.dot_general(kd, qg, (((1,), (1,)), ((), ())), preferred_element_type=F32)
            s = s + bias_ref[first, kvh]
            m = jnp.maximum(jnp.max(s, axis=0, keepdims=True), sink)
            e = jnp.exp2(s - m).astype(BF16)
            ox = jnp.dot(vtx, e, preferred_element_type=F32)
            denom = ox[2 * SWA_HEAD_DIM:2 * SWA_HEAD_DIM + 1, :] + jnp.exp2(sink - m)
            o = ox[:2 * SWA_HEAD_DIM, :] / denom
            for j in range(SWA_GROUP // 2):
                a = o[:, 2 * j * BLOCK:(2 * j + 1) * BLOCK].T
                b = o[:, (2 * j + 1) * BLOCK:(2 * j + 2) * BLOCK].T
                outs.append(jnp.where(low, a, b).astype(BF16))
        o_ref[qb * BLOCK:(qb + 1) * BLOCK, :] = jnp.concatenate(outs, axis=1)


def _swa(z, bias, sinks, batch, seq):
    nqb = 2
    rows = nqb * BLOCK
    ns = seq // rows
    t = batch * seq
    qcols = SWA_HEADS * SWA_HEAD_DIM

    def cur(col):
        return pl.BlockSpec((rows, SWA_KV), lambda b, n: (b * ns + n, col))

    def prev(col):
        return pl.BlockSpec((BLOCK, SWA_KV),
                            lambda b, n: ((b * ns + n) * nqb - jnp.minimum(n, 1), col))

    return pl.pallas_call(
        functools.partial(_swa_kernel, nqb=nqb),
        grid=(batch, ns),
        in_specs=[
            pl.BlockSpec(memory_space=pltpu.SMEM),
            pl.BlockSpec((rows, qcols), lambda b, n: (b * ns + n, Z_QS // qcols)),
            prev(Z_KS // SWA_KV), cur(Z_KS // SWA_KV),
            prev(Z_VS // SWA_KV), cur(Z_VS // SWA_KV),
            pl.BlockSpec(bias.shape, lambda b, n: (0, 0, 0, 0), pipeline_mode=pl.Buffered(1)),
        ],
        out_specs=pl.BlockSpec((rows, qcols), lambda b, n: (b * ns + n, 0)),
        out_shape=jax.ShapeDtypeStruct((t, qcols), BF16),
        compiler_params=_cparams(("parallel", "arbitrary")),
        name="swa",
    )(sinks, z, z, z, z, z, bias)


def _mix_kernel(oa_ref, ob_ref, ga_ref, gb_ref, w_ref, x_ref, g_ref, mod_ref, o_ref, *, tm, sub):
    for r in range(tm // sub):
        rows = slice(r * sub, (r + 1) * sub)
        ga = jax.nn.sigmoid(ga_ref[rows, :].astype(F32))
        gb = jax.nn.sigmoid(gb_ref[rows, :].astype(F32))
        u = ga * oa_ref[rows, :].astype(F32) + gb * ob_ref[rows, :].astype(F32)
        mix = jnp.dot(u.astype(BF16), w_ref[...], preferred_element_type=F32)
        o_ref[rows, :] = x_ref[rows, :] + mod_ref[GT1:GT1 + 1, :] * _rms(mix, g_ref[...])


def _mix(oa, ob, z, w, x2, g, mod, seq):
    t, d = x2.shape
    tm, sub = 512, 256
    tpb = seq // tm
    row = lambda i: (i, 0)
    return pl.pallas_call(
        functools.partial(_mix_kernel, tm=tm, sub=sub),
        grid=(t // tm,),
        in_specs=[
            pl.BlockSpec((tm, d), row),
            pl.BlockSpec((tm, d), row),
            pl.BlockSpec((tm, d), lambda i: (i, Z_GA // d)),
            pl.BlockSpec((tm, d), lambda i: (i, Z_GB // d)),
            pl.BlockSpec((d, d), lambda i: (0, 0), pipeline_mode=pl.Buffered(1)),
            pl.BlockSpec((tm, d), row),
            pl.BlockSpec((1, d), lambda i: (0, 0)),
            pl.BlockSpec((None, 6, d), lambda i: (i // tpb, 0, 0)),
        ],
        out_specs=pl.BlockSpec((tm, d), row),
        out_shape=jax.ShapeDtypeStruct((t, d), F32),
        compiler_params=_cparams(("parallel",)),
        name="mix",
    )(oa, ob, z, z, w, x2, g, mod)


def _ffn_up_kernel(x_ref, halo_ref, g_ref, mod_ref, wg_ref, wv_ref, cwg_ref, cwv_ref,
                   cbg_ref, cbv_ref, o_ref, h_ref, *, tm, tn, sub, tpb):
    i = pl.program_id(0)

    @pl.when(pl.program_id(1) == 0)
    def _():
        g = g_ref[...]
        sc = mod_ref[SC2:SC2 + 1, :]
        sh = mod_ref[SH2:SH2 + 1, :]
        halo = (_rms(halo_ref[...], g) * (1.0 + sc) + sh).astype(BF16)
        h_ref[0:BF16_ROWS, :] = jnp.where(i % tpb == 0, jnp.zeros_like(halo), halo)
        _fill_normed(x_ref, g, sc, sh, h_ref, BF16_ROWS, tm, 128)

    h = h_ref[...]

    def conv(w_ref, cw_ref, cb_ref, cols):
        u = jnp.dot(h, w_ref[:, cols], preferred_element_type=F32)
        y = cw_ref[1:2, cols] * u + pltpu.roll(cw_ref[0:1, cols] * u, 1, 0)
        y = cw_ref[2:3, cols] * u + pltpu.roll(y, 1, 0)
        return cb_ref[:, cols] + y[BF16_ROWS:, :]

    for c in range(tn // sub):
        cols = slice(c * sub, (c + 1) * sub)
        gate = conv(wg_ref, cwg_ref, cbg_ref, cols)
        val = conv(wv_ref, cwv_ref, cbv_ref, cols)
        o_ref[:, cols] = (gate * jax.nn.sigmoid(gate) * val).astype(o_ref.dtype)


def _ffn_up(x1, g, mod, w_up, conv_w, conv_b, seq):
    t, d = x1.shape
    tm, tn, sub = 1024, 512, 256
    tpb = seq // tm
    nj = D_FF // tn
    hb = tm // BF16_ROWS
    return pl.pallas_call(
        functools.partial(_ffn_up_kernel, tm=tm, tn=tn, sub=sub, tpb=tpb),
        grid=(t // tm, nj),
        in_specs=[
            pl.BlockSpec((tm, d), lambda i, j: (i, 0)),
            pl.BlockSpec((BF16_ROWS, d), lambda i, j: (jnp.maximum(i * hb - 1, 0), 0)),
            pl.BlockSpec((1, d), lambda i, j: (0, 0)),
            pl.BlockSpec((None, 6, d), lambda i, j: (i // tpb, 0, 0)),
            pl.BlockSpec((d, tn), lambda i, j: (0, j)),
            pl.BlockSpec((d, tn), lambda i, j: (0, j + nj)),
            pl.BlockSpec((CONV_WIDTH, tn), lambda i, j: (0, j)),
            pl.BlockSpec((CONV_WIDTH, tn), lambda i, j: (0, j + nj)),
            pl.BlockSpec((1, tn), lambda i, j: (0, j)),
            pl.BlockSpec((1, tn), lambda i, j: (0, j + nj)),
        ],
        out_specs=pl.BlockSpec((tm, tn), lambda i, j: (i, j)),
        out_shape=jax.ShapeDtypeStruct((t, D_FF), BF16),
        scratch_shapes=[pltpu.VMEM((tm + BF16_ROWS, d), BF16)],
        compiler_params=_cparams(("parallel", "arbitrary")),
        name="ffn_up",
    )(x1, x1, g, mod, w_up, w_up, conv_w, conv_w, conv_b, conv_b)


def _ffn_down_kernel(a_ref, w_ref, x_ref, g_ref, mod_ref, o_ref):
    y = jnp.dot(a_ref[...], w_ref[...], preferred_element_type=F32)
    o_ref[...] = x_ref[...] + mod_ref[GT2:GT2 + 1, :] * _rms(y, g_ref[...])


def _ffn_down(a, w, x1, g, mod, seq):
    t, d = x1.shape
    tm = 256
    tpb = seq // tm
    return pl.pallas_call(
        _ffn_down_kernel,
        grid=(t // tm,),
        in_specs=[
            pl.BlockSpec((tm, D_FF), lambda i: (i, 0)),
            pl.BlockSpec((D_FF, d), lambda i: (0, 0), pipeline_mode=pl.Buffered(1)),
            pl.BlockSpec((tm, d), lambda i: (i, 0)),
            pl.BlockSpec((1, d), lambda i: (0, 0)),
            pl.BlockSpec((None, 6, d), lambda i: (i // tpb, 0, 0)),
        ],
        out_specs=pl.BlockSpec((tm, d), lambda i: (i, 0)),
        out_shape=jax.ShapeDtypeStruct((t, d), F32),
        compiler_params=_cparams(("parallel",)),
        name="ffn_down",
    )(a, w, x1, g, mod)


def _rot_cols(w):
    half = MLA_ROPE // 2
    w = w.reshape(w.shape[0], -1, MLA_ROPE)
    return jnp.concatenate([-w[..., half:], w[..., :half]], axis=-1).reshape(w.shape[0], -1)


def _prep_w_in(w_in):
    o = 0
    cq = w_in[:, o:o + MLA_Q_RANK]; o += MLA_Q_RANK
    ckv = w_in[:, o:o + MLA_KV_RANK]; o += MLA_KV_RANK
    kr = w_in[:, o:o + MLA_ROPE]; o += MLA_ROPE
    qs = w_in[:, o:o + D_MODEL]; o += D_MODEL
    ks = w_in[:, o:o + SWA_KV]; o += SWA_KV
    vs = w_in[:, o:o + SWA_KV]; o += SWA_KV
    gates = w_in[:, o:o + 2 * D_MODEL]
    krr = _rot_cols(kr)
    w = jnp.concatenate([qs * (SWA_HEAD_DIM ** -0.5 * LOG2E), gates, cq, ks, ckv, vs, kr, kr, krr, krr],
                        axis=1)
    assert w.shape[1] == Z_COLS
    return w.astype(BF16)


def _prep_w_uq(w_uq):
    w = (w_uq * (MLA_QK ** -0.5 * LOG2E)).reshape(MLA_Q_RANK, MLA_HEADS, MLA_QK)
    wn = w[..., :MLA_NOPE].reshape(MLA_Q_RANK, -1)
    wr = w[..., MLA_NOPE:].reshape(MLA_Q_RANK, -1)
    return wn.astype(BF16), wr.astype(BF16), _rot_cols(wr).astype(BF16)


def _prep_w_ukv(w_ukv):
    w = w_ukv.reshape(MLA_KV_RANK, MLA_HEADS, MLA_NOPE + MLA_V)
    wk = w[..., :MLA_NOPE].reshape(MLA_KV_RANK, -1)
    wvt = w[..., MLA_NOPE:].reshape(MLA_KV_RANK, -1).T
    return wk.astype(BF16), wvt.astype(BF16)


def _rope_tables2(seq):
    pos = jnp.arange(seq, dtype=F32)
    inv = ROPE_THETA ** (-jnp.arange(0, MLA_ROPE, 2, dtype=F32) / MLA_ROPE)
    ang = pos[:, None] * inv[None, :]
    ang = jnp.concatenate([ang, ang, ang, ang], axis=-1)
    return jnp.cos(ang), jnp.sin(ang)


def kernel(x, c, w_ada, b_ada, g_pre_mix, g_post_mix, w_in, g_q_lat, w_uq, g_kv_lat, w_ukv,
           rel_bias, sinks, w_o, g_pre_ffn, g_post_ffn, w_up, conv_w, conv_b, w_down):
    batch, seq, d = x.shape
    depth = w_ada.shape[0]
    xt = x.reshape(batch * seq, d)
    c8 = jnp.pad(c, ((0, 8 - batch), (0, 0)))
    cos2, sin2 = _rope_tables2(seq)
    bias = _t5bias(rel_bias)
    for l in range(depth):
        mod = _ada(c8, w_ada[l], b_ada[l][None, :])[:batch].reshape(batch, 6, d)
        z = _inproj(xt, g_pre_mix[l][None, :], mod, _prep_w_in(w_in[l]), seq)
        wn, wr, wrr = _prep_w_uq(w_uq[l])
        qn, qr = _qproj(z, g_q_lat[l][None, :], wn, wr, wrr, cos2, sin2, seq)
        wk, wvt = _prep_w_ukv(w_ukv[l])
        kn, vt, kr2 = _kvproj(z, g_kv_lat[l][None, :], wk, wvt, cos2, sin2, batch, seq)
        o_a = _mla(qn, qr, kn, kr2, vt, batch, seq)
        o_b = _swa(z, bias, sinks[l], batch, seq)
        x1 = _mix(o_a, o_b, z, w_o[l].astype(BF16), xt, g_post_mix[l][None, :], mod, seq)
        a = _ffn_up(x1, g_pre_ffn[l][None, :], mod, w_up[l].astype(BF16), conv_w[l],
                    conv_b[l][None, :], seq)
        xt = _ffn_down(a, w_down[l].astype(BF16), x1, g_post_ffn[l][None, :], mod, seq)
    return xt.reshape(batch, seq, d)
```

```python
import functools
import math

import jax
import jax.numpy as jnp
from jax import lax
from jax.experimental import pallas as pl
from jax.experimental.pallas import tpu as pltpu

F32 = jnp.float32
BF16 = jnp.bfloat16

D_MODEL = 2048
MLA_NOPE = 128
MLA_ROPE = 64
MLA_V = 128
MLA_HEADS = D_MODEL // MLA_V
MLA_Q_RANK = 768
MLA_KV_RANK = 512
MLA_QK = MLA_NOPE + MLA_ROPE
ROPE_THETA = 10000.0
SWA_HEAD_DIM = 64
SWA_HEADS = D_MODEL // SWA_HEAD_DIM
SWA_KV_HEADS = 4
SWA_GROUP = SWA_HEADS // SWA_KV_HEADS
SWA_KV = SWA_KV_HEADS * SWA_HEAD_DIM
WINDOW = 128
BLOCK = 128
REL_BUCKETS = 32
REL_MAX_DIST = 128
D_FF = 5632
CONV_WIDTH = 3
EPS = 1e-6
NEG = -1e30
LOG2E = 1.4426950408889634

VMEM_LIMIT_BYTES = 56 * 1024 * 1024
LANES = 128
BF16_ROWS = 16
MLA_TILE = 512

Z_CQ = 0
Z_CKV = 768
Z_KR = 1280
Z_KRR = 1408
Z_QS = 1536
Z_KS = 3584
Z_VS = 3840
Z_GA = 4096
Z_GB = 6144
Z_COLS = 8192
W_IN_KR = MLA_Q_RANK + MLA_KV_RANK
W_IN_QS = W_IN_KR + MLA_ROPE

SH1, SC1, GT1, SH2, SC2, GT2 = range(6)


def _cparams(sem, flags=None):
    return pltpu.CompilerParams(dimension_semantics=sem, vmem_limit_bytes=VMEM_LIMIT_BYTES,
                                flags=flags)


def _rms(x, g):
    ms = jnp.mean(x * x, axis=-1, keepdims=True)
    return x * lax.rsqrt(ms + EPS) * g


def _ada_kernel(c_ref, w_ref, b_ref, o_ref):
    c = c_ref[...]
    ca = (c * jax.nn.sigmoid(c)).astype(BF16)
    o_ref[...] = jnp.dot(ca, w_ref[...].astype(BF16), preferred_element_type=F32) + b_ref[...]


def _ada(c8, w_ada, b_ada):
    n = w_ada.shape[1]
    tn = 1024
    return pl.pallas_call(
        _ada_kernel,
        grid=(n // tn,),
        in_specs=[
            pl.BlockSpec((8, D_MODEL), lambda j: (0, 0)),
            pl.BlockSpec((D_MODEL, tn), lambda j: (0, j)),
            pl.BlockSpec((1, tn), lambda j: (0, j)),
        ],
        out_specs=pl.BlockSpec((8, tn), lambda j: (0, j)),
        out_shape=jax.ShapeDtypeStruct((8, n), F32),
        compiler_params=_cparams(("arbitrary",)),
        name="ada",
    )(c8, w_ada, b_ada)


def _fill_normed(x_ref, g, sc, sh, h_ref, row0, nrows, chunk):
    def body(i, carry):
        r = pl.multiple_of(i * chunk, chunk)
        y = _rms(x_ref[pl.ds(r, chunk), :], g)
        h_ref[pl.ds(row0 + r, chunk), :] = (y * (1.0 + sc) + sh).astype(BF16)
        return carry

    lax.fori_loop(0, nrows // chunk, body, 0)


def _prenorm_kernel(x_ref, g_ref, mod_ref, o_ref, *, tm):
    _fill_normed(x_ref, g_ref[...], mod_ref[SC1:SC1 + 1, :], mod_ref[SH1:SH1 + 1, :],
                 o_ref, 0, tm, 128)


def _prenorm(x2, g, mod, seq):
    t, d = x2.shape
    tm = 1024
    tpb = seq // tm
    return pl.pallas_call(
        functools.partial(_prenorm_kernel, tm=tm),
        grid=(t // tm,),
        in_specs=[
            pl.BlockSpec((tm, d), lambda i: (i, 0)),
            pl.BlockSpec((1, d), lambda i: (0, 0)),
            pl.BlockSpec((None, 6, d), lambda i: (i // tpb, 0, 0)),
        ],
        out_specs=pl.BlockSpec((tm, d), lambda i: (i, 0)),
        out_shape=jax.ShapeDtypeStruct((t, d), BF16),
        compiler_params=_cparams(("parallel",)),
        name="prenorm",
    )(x2, g, mod)


INPROJ_TN = 512
KR_TILE = Z_KR // INPROJ_TN
QS_TILES = (Z_QS // INPROJ_TN, Z_KS // INPROJ_TN)


def _inproj_kernel(h_ref, wt_ref, o_ref, wb_ref):
    j = pl.program_id(0)

    @pl.when(pl.program_id(1) == 0)
    def _():
        is_qs = (j >= QS_TILES[0]) & (j < QS_TILES[1])
        scale = jnp.where(is_qs, SWA_HEAD_DIM ** -0.5 * LOG2E, 1.0).astype(F32)
        wb_ref[...] = (wt_ref[...] * scale).astype(BF16)

        @pl.when(j == KR_TILE)
        def _():
            lo = Z_KR - KR_TILE * INPROJ_TN
            kr = wt_ref[lo:lo + MLA_ROPE, :]
            half = MLA_ROPE // 2
            rot = jnp.concatenate([-kr[half:, :], kr[:half, :]], axis=0)
            wb_ref[lo:, :] = jnp.concatenate([kr, kr, rot, rot], axis=0).astype(BF16)

    o_ref[...] = lax.dot_general(h_ref[...], wb_ref[...], (((1,), (1,)), ((), ())),
                                 preferred_element_type=F32).astype(o_ref.dtype)


def _inproj(h, wt):
    t, d = h.shape
    tm, tn = 2048, INPROJ_TN

    def src_row(j):
        u = MLA_ROPE
        return jnp.where(j <= KR_TILE, j * (tn // u), W_IN_QS // u + (j - KR_TILE - 1) * (tn // u)) * u

    return pl.pallas_call(
        _inproj_kernel,
        grid=(Z_COLS // tn, t // tm),
        in_specs=[
            pl.BlockSpec((tm, d), lambda j, i: (i, 0)),
            pl.BlockSpec((pl.Element(tn), pl.Element(d)), lambda j, i: (src_row(j), 0)),
        ],
        out_specs=pl.BlockSpec((tm, tn), lambda j, i: (i, j)),
        out_shape=jax.ShapeDtypeStruct((t, Z_COLS), BF16),
        scratch_shapes=[pltpu.VMEM((tn, d), BF16)],
        compiler_params=_cparams(("arbitrary", "arbitrary")),
        name="inproj",
    )(h, wt)


def _rope_chunks(a, b, cos, sin):
    n = a.shape[1] // LANES
    out = [a[:, LANES * i:LANES * (i + 1)] * cos + b[:, LANES * i:LANES * (i + 1)] * sin
           for i in range(n)]
    return out[0] if n == 1 else jnp.concatenate(out, axis=1)


def _qproj_kernel(cq_ref, g_ref, wn_ref, wr_ref, wrr_ref, cos_ref, sin_ref, qn_ref, qr_ref):
    cqn = _rms(cq_ref[...].astype(F32), g_ref[...]).astype(BF16)
    qn_ref[...] = jnp.dot(cqn, wn_ref[...], preferred_element_type=F32).astype(BF16)
    a = jnp.dot(cqn, wr_ref[...], preferred_element_type=F32)
    b = jnp.dot(cqn, wrr_ref[...], preferred_element_type=F32)
    qr_ref[...] = _rope_chunks(a, b, cos_ref[...], sin_ref[...]).astype(BF16)


def _qproj(z, g, wn, wr, wrr, cos2, sin2, seq):
    t = z.shape[0]
    tm = 512
    spb = seq // tm
    hn = MLA_HEADS * MLA_NOPE
    hr = MLA_HEADS * MLA_ROPE
    return pl.pallas_call(
        _qproj_kernel,
        grid=(t // tm,),
        in_specs=[
            pl.BlockSpec((tm, MLA_Q_RANK), lambda i: (i, Z_CQ // MLA_Q_RANK)),
            pl.BlockSpec((1, MLA_Q_RANK), lambda i: (0, 0)),
            pl.BlockSpec((MLA_Q_RANK, hn), lambda i: (0, 0)),
            pl.BlockSpec((MLA_Q_RANK, hr), lambda i: (0, 0)),
            pl.BlockSpec((MLA_Q_RANK, hr), lambda i: (0, 0)),
            pl.BlockSpec((tm, LANES), lambda i: (i % spb, 0)),
            pl.BlockSpec((tm, LANES), lambda i: (i % spb, 0)),
        ],
        out_specs=[
            pl.BlockSpec((tm, hn), lambda i: (i, 0)),
            pl.BlockSpec((tm, hr), lambda i: (i, 0)),
        ],
        out_shape=[jax.ShapeDtypeStruct((t, hn), BF16), jax.ShapeDtypeStruct((t, hr), BF16)],
        compiler_params=_cparams(("parallel",)),
        name="qproj",
    )(z, g, wn, wr, wrr, cos2, sin2)


def _kvproj_kernel(ckv_ref, g_ref, wk_ref, wvt_ref, kr_ref, krr_ref, cos_ref, sin_ref,
                   kn_ref, vt_ref, kro_ref):
    ckvn = _rms(ckv_ref[...].astype(F32), g_ref[...]).astype(BF16)
    kn_ref[...] = jnp.dot(ckvn, wk_ref[...], preferred_element_type=F32).astype(BF16)
    vt = lax.dot_general(wvt_ref[...], ckvn, (((1,), (1,)), ((), ())), preferred_element_type=F32)
    vt_ref[...] = vt.reshape(vt_ref.shape).astype(BF16)
    kro_ref[...] = (kr_ref[...].astype(F32) * cos_ref[...]
                    + krr_ref[...].astype(F32) * sin_ref[...]).astype(BF16)


def _kvproj(z, g, wk, wvt, cos2, sin2, batch, seq):
    t = z.shape[0]
    tm = MLA_TILE
    spb = seq // tm
    hn = MLA_HEADS * MLA_NOPE
    hv = MLA_HEADS * MLA_V
    return pl.pallas_call(
        _kvproj_kernel,
        grid=(t // tm,),
        in_specs=[
            pl.BlockSpec((pl.Element(tm), pl.Element(MLA_KV_RANK)), lambda i: (i * tm, Z_CKV)),
            pl.BlockSpec((1, MLA_KV_RANK), lambda i: (0, 0)),
            pl.BlockSpec((MLA_KV_RANK, hn), lambda i: (0, 0)),
            pl.BlockSpec((hv, MLA_KV_RANK), lambda i: (0, 0)),
            pl.BlockSpec((tm, LANES), lambda i: (i, Z_KR // LANES)),
            pl.BlockSpec((tm, LANES), lambda i: (i, Z_KRR // LANES)),
            pl.BlockSpec((tm, LANES), lambda i: (i % spb, 0)),
            pl.BlockSpec((tm, LANES), lambda i: (i % spb, 0)),
        ],
        out_specs=[
            pl.BlockSpec((tm, hn), lambda i: (i, 0)),
            pl.BlockSpec((None, MLA_HEADS, None, MLA_V, tm), lambda i: (i // spb, 0, i % spb, 0, 0)),
            pl.BlockSpec((tm, LANES), lambda i: (i, 0)),
        ],
        out_shape=[jax.ShapeDtypeStruct((t, hn), BF16),
                   jax.ShapeDtypeStruct((batch, MLA_HEADS, spb, MLA_V, tm), BF16),
                   jax.ShapeDtypeStruct((t, LANES), BF16)],
        compiler_params=_cparams(("parallel",)),
        name="kvproj",
    )(z, g, wk, wvt, z, z, cos2, sin2)


def _mla_kernel(qn_ref, qr_ref, kn_ref, kr_ref, vt_ref, o_ref, q_ref, s_ref, p_ref, acc_ref,
                *, tile, nq):
    head = pl.program_id(1)
    lo = (head % 2) * MLA_ROPE

    def scores(q, k):
        k0 = k * tile if isinstance(k, int) else pl.multiple_of(k * tile, tile)
        kc = jnp.concatenate([kn_ref[pl.ds(k0, tile), :], kr_ref[pl.ds(k0, tile), :]], axis=1)
        return lax.dot_general(kc, q, (((1,), (1,)), ((), ())), preferred_element_type=F32)

    def softmax(s, m, l):
        m_new = jnp.maximum(m, jnp.max(s, axis=0, keepdims=True))
        alpha = jnp.exp2(m - m_new)
        p = jnp.exp2(s - m_new)
        return p.astype(BF16), alpha, m_new, alpha * l + jnp.sum(p, axis=0, keepdims=True)

    def values(p, k):
        return jnp.dot(vt_ref[k], p, preferred_element_type=F32)

    for qi in range(nq):
        rows = slice(qi * tile, (qi + 1) * tile)
        qr = qr_ref[rows, :]
        lane = lax.broadcasted_iota(jnp.int32, qr.shape, 1)
        keep = (lane >= lo) & (lane < lo + MLA_ROPE)
        q_ref[...] = jnp.concatenate(
            [qn_ref[rows, :], jnp.where(keep, qr, jnp.zeros_like(qr))], axis=1)

        s_ref[0] = scores(q_ref[...], 0)
        m = jnp.full((1, tile), NEG, F32)
        l = jnp.zeros((1, tile), F32)
        for k in range(qi + 1):
            if k < qi:
                s_ref[(k + 1) % 3] = scores(q_ref[...], k + 1)
            s = s_ref[k % 3]
            if k == qi:
                key = lax.broadcasted_iota(jnp.int32, s.shape, 0)
                qry = lax.broadcasted_iota(jnp.int32, s.shape, 1)
                s = jnp.where(key <= qry, s, NEG)
            p, a, m, l = softmax(s, m, l)
            p_ref[k % 2] = p
            pv = values(p_ref[k % 2], k)
            acc_ref[...] = pv if k == 0 else a * acc_ref[...] + pv
        o_ref[rows, :] = (acc_ref[...] / l).T.astype(o_ref.dtype)


def _mla(qn, qr, kn, kr2, vt, batch, seq):
    tile = MLA_TILE
    nq = seq // tile
    t = batch * seq
    tok = lambda b, h: (b, h)
    return pl.pallas_call(
        functools.partial(_mla_kernel, tile=tile, nq=nq),
        grid=(batch, MLA_HEADS),
        in_specs=[
            pl.BlockSpec((seq, LANES), tok),
            pl.BlockSpec((seq, LANES), lambda b, h: (b, h // 2)),
            pl.BlockSpec((seq, LANES), tok),
            pl.BlockSpec((seq, LANES), lambda b, h: (b, 0)),
            pl.BlockSpec((None, None, nq, MLA_V, tile), lambda b, h: (b, h, 0, 0, 0)),
        ],
        out_specs=pl.BlockSpec((seq, LANES), tok),
        out_shape=jax.ShapeDtypeStruct((t, MLA_HEADS * MLA_V), BF16),
        scratch_shapes=[
            pltpu.VMEM((tile, 2 * LANES), BF16),
            pltpu.VMEM((3, tile, tile), F32),
            pltpu.VMEM((2, tile, tile), BF16),
            pltpu.VMEM((MLA_V, tile), F32),
        ],
        compiler_params=_cparams(("parallel", "arbitrary")),
        name="mla",
    )(qn, qr, kn, kr2, vt)


def _t5_bucket(dist):
    max_exact = REL_BUCKETS // 2
    n = jnp.maximum(dist, 0)
    large = max_exact + (jnp.log(jnp.maximum(n, 1).astype(F32) / max_exact)
                         / math.log(REL_MAX_DIST / max_exact)
                         * (REL_BUCKETS - max_exact)).astype(jnp.int32)
    large = jnp.minimum(large, REL_BUCKETS - 1)
    return jnp.where(n < max_exact, n, large)


def _t5bias_kernel(rb_ref, bucket_ref, o_ref):
    kvh = pl.program_id(0)
    bucket = bucket_ref[...]
    key = lax.broadcasted_iota(jnp.int32, bucket.shape, 0)
    qry = lax.broadcasted_iota(jnp.int32, bucket.shape, 1)
    dist = BLOCK + qry - key
    mask = (dist >= 0) & (dist < WINDOW)
    mask_first = mask & (key >= BLOCK)
    for g in range(SWA_GROUP):
        acc = jnp.zeros(bucket.shape, F32)
        for b in range(REL_BUCKETS):
            acc = jnp.where(bucket == b, rb_ref[b, kvh * SWA_GROUP + g], acc)
        cols = slice(g * BLOCK, (g + 1) * BLOCK)
        o_ref[0, :, cols] = jnp.where(mask, acc * LOG2E, NEG)
        o_ref[1, :, cols] = jnp.where(mask_first, acc * LOG2E, NEG)


def _t5bias(rel_bias):
    a = jnp.arange(BLOCK)
    bidx = jnp.arange(2 * BLOCK)
    bucket = _t5_bucket(BLOCK + a[None, :] - bidx[:, None]).astype(jnp.int32)
    return pl.pallas_call(
        _t5bias_kernel,
        grid=(SWA_KV_HEADS,),
        in_specs=[
            pl.BlockSpec(memory_space=pltpu.SMEM),
            pl.BlockSpec((2 * BLOCK, BLOCK), lambda h: (0, 0)),
        ],
        out_specs=pl.BlockSpec((2, None, 2 * BLOCK, SWA_GROUP * BLOCK), lambda h: (0, h, 0, 0)),
        out_shape=jax.ShapeDtypeStruct((2, SWA_KV_HEADS, 2 * BLOCK, SWA_GROUP * BLOCK), F32),
        compiler_params=_cparams(("arbitrary",)),
        name="t5bias",
    )(rel_bias, bucket)


def _roll_half(x):
    return pltpu.bitcast(pltpu.roll(pltpu.bitcast(x, jnp.uint32), LANES // 2, 1), x.dtype)


def _keep_lanes(x, word_mask):
    return pltpu.bitcast(pltpu.bitcast(x, jnp.uint32) & word_mask, x.dtype)


def _swa_kernel(sink_ref, q_ref, kp_ref, kc_ref, vp_ref, vc_ref, bias_ref, o_ref, *, nqb):
    seq_start = jnp.where(pl.program_id(1) == 0, 1, 0)
    k_all = jnp.concatenate([kp_ref[...], kc_ref[...]], axis=0)
    v_all = jnp.concatenate([vp_ref[...], vc_ref[...]], axis=0)
    vt_all = v_all.astype(F32).T.astype(BF16)
    low = lax.broadcasted_iota(jnp.int32, (BLOCK, LANES), 1) < SWA_HEAD_DIM
    low2 = lax.broadcasted_iota(jnp.int32, (2 * BLOCK, LANES), 1) < SWA_HEAD_DIM
    ones = jnp.ones((BF16_ROWS, 2 * BLOCK), BF16)
    lane = lax.broadcasted_iota(jnp.int32, (BLOCK // 2, LANES), 1)
    keep_low = jnp.where(lane < SWA_HEAD_DIM, jnp.uint32(0xFFFFFFFF), jnp.uint32(0))
    keep_high = ~keep_low
    for qb in range(nqb):
        first = seq_start if qb == 0 else 0
        q = q_ref[qb * BLOCK:(qb + 1) * BLOCK, :]
        k = k_all[qb * BLOCK:(qb + 2) * BLOCK, :]
        vt = vt_all[:, qb * BLOCK:(qb + 2) * BLOCK]
        outs = []
        for kvh in range(SWA_KV_HEADS):
            pair = slice(LANES * (kvh // 2), LANES * (kvh // 2 + 1))
            kb = k[:, pair]
            kr = _roll_half(kb)
            kd = jnp.where(low2, kr, kb) if kvh % 2 else jnp.where(low2, kb, kr)
            vth = vt[SWA_HEAD_DIM * kvh:SWA_HEAD_DIM * (kvh + 1), :]
            vtx = jnp.concatenate([vth, vth, ones], axis=0)
            qs, sinks = [], []
            for g in range(SWA_GROUP):
                hq = kvh * SWA_GROUP + g
                blk = q[:, LANES * (hq // 2):LANES * (hq // 2 + 1)]
                qs.append(_keep_lanes(blk, keep_low if hq % 2 == 0 else keep_high))
                sinks.append(jnp.full((1, BLOCK), sink_ref[hq] * LOG2E, F32))
            qg = jnp.concatenate(qs, axis=0)
            sink = jnp.concatenate(sinks, axis=1)
            s = lax.dot_general(kd, qg, (((1,), (1,)), ((), ())), preferred_element_type=F32)
            s = s + bias_ref[first, kvh]
            m = jnp.maximum(jnp.max(s, axis=0, keepdims=True), sink)
            e = jnp.exp2(s - m).astype(BF16)
            ox = jnp.dot(vtx, e, preferred_element_type=F32)
            denom = ox[2 * SWA_HEAD_DIM:2 * SWA_HEAD_DIM + 1, :] + jnp.exp2(sink - m)
            o = ox[:2 * SWA_HEAD_DIM, :] / denom
            for j in range(SWA_GROUP // 2):
                a = o[:, 2 * j * BLOCK:(2 * j + 1) * BLOCK].T
                b = o[:, (2 * j + 1) * BLOCK:(2 * j + 2) * BLOCK].T
                outs.append(jnp.where(low, a, b).astype(BF16))
        o_ref[qb * BLOCK:(qb + 1) * BLOCK, :] = jnp.concatenate(outs, axis=1)


def _swa(z, bias, sinks, batch, seq):
    nqb = 2
    rows = nqb * BLOCK
    ns = seq // rows
    t = batch * seq
    qcols = SWA_HEADS * SWA_HEAD_DIM

    def cur(col):
        return pl.BlockSpec((rows, SWA_KV), lambda b, n: (b * ns + n, col))

    def prev(col):
        return pl.BlockSpec((BLOCK, SWA_KV),
                            lambda b, n: ((b * ns + n) * nqb - jnp.minimum(n, 1), col))

    return pl.pallas_call(
        functools.partial(_swa_kernel, nqb=nqb),
        grid=(batch, ns),
        in_specs=[
            pl.BlockSpec(memory_space=pltpu.SMEM),
            pl.BlockSpec((pl.Element(rows), pl.Element(qcols)),
                         lambda b, n: ((b * ns + n) * rows, Z_QS)),
            prev(Z_KS // SWA_KV), cur(Z_KS // SWA_KV),
            prev(Z_VS // SWA_KV), cur(Z_VS // SWA_KV),
            pl.BlockSpec(bias.shape, lambda b, n: (0, 0, 0, 0), pipeline_mode=pl.Buffered(1)),
        ],
        out_specs=pl.BlockSpec((rows, qcols), lambda b, n: (b * ns + n, 0)),
        out_shape=jax.ShapeDtypeStruct((t, qcols), BF16),
        compiler_params=_cparams(("parallel", "arbitrary")),
        name="swa",
    )(sinks, z, z, z, z, z, bias)


def _mix_kernel(oa_ref, ob_ref, ga_ref, gb_ref, w_ref, x_ref, g_ref, g2_ref, mod_ref, o_ref, h2_ref,
                *, tm, sub):
    for r in range(tm // sub):
        rows = slice(r * sub, (r + 1) * sub)
        ga = jax.nn.sigmoid(ga_ref[rows, :].astype(F32))
        gb = jax.nn.sigmoid(gb_ref[rows, :].astype(F32))
        u = ga * oa_ref[rows, :].astype(F32) + gb * ob_ref[rows, :].astype(F32)
        mix = jnp.dot(u.astype(BF16), w_ref[...], preferred_element_type=F32)
        x1 = x_ref[rows, :] + mod_ref[GT1:GT1 + 1, :] * _rms(mix, g_ref[...])
        o_ref[rows, :] = x1
        h2 = _rms(x1, g2_ref[...]) * (1.0 + mod_ref[SC2:SC2 + 1, :]) + mod_ref[SH2:SH2 + 1, :]
        h2_ref[rows, :] = h2.astype(BF16)


def _mix(oa, ob, z, w, x2, g, g2, mod, seq):
    t, d = x2.shape
    tm, sub = 512, 256
    tpb = seq // tm
    row = lambda i: (i, 0)
    return pl.pallas_call(
        functools.partial(_mix_kernel, tm=tm, sub=sub),
        grid=(t // tm,),
        in_specs=[
            pl.BlockSpec((tm, d), row),
            pl.BlockSpec((tm, d), row),
            pl.BlockSpec((tm, d), lambda i: (i, Z_GA // d)),
            pl.BlockSpec((tm, d), lambda i: (i, Z_GB // d)),
            pl.BlockSpec((d, d), lambda i: (0, 0), pipeline_mode=pl.Buffered(1)),
            pl.BlockSpec((tm, d), row),
            pl.BlockSpec((1, d), lambda i: (0, 0)),
            pl.BlockSpec((1, d), lambda i: (0, 0)),
            pl.BlockSpec((None, 6, d), lambda i: (i // tpb, 0, 0)),
        ],
        out_specs=[pl.BlockSpec((tm, d), row), pl.BlockSpec((tm, d), row)],
        out_shape=[jax.ShapeDtypeStruct((t, d), F32), jax.ShapeDtypeStruct((t, d), BF16)],
        compiler_params=_cparams(("parallel",)),
        name="mix",
    )(oa, ob, z, z, w, x2, g, g2, mod)


def _ffn_up_kernel(h2_ref, halo_ref, wg32_ref, wv32_ref, cwg_ref, cwv_ref, cbg_ref, cbv_ref,
                   o_ref, h_ref, wg_ref, wv_ref, *, tm, tn, sub, tpb):
    i = pl.program_id(1)

    @pl.when(i == 0)
    def _():
        wg_ref[...] = wg32_ref[...].astype(BF16)
        wv_ref[...] = wv32_ref[...].astype(BF16)

    halo = halo_ref[...]
    h_ref[0:BF16_ROWS, :] = jnp.where(i % tpb == 0, jnp.zeros_like(halo), halo)
    h_ref[BF16_ROWS:, :] = h2_ref[...]
    h = h_ref[...]

    def conv(w_ref, cw_ref, cb_ref, cols):
        u = jnp.dot(h, w_ref[:, cols], preferred_element_type=F32)
        y = cw_ref[1:2, cols] * u + pltpu.roll(cw_ref[0:1, cols] * u, 1, 0)
        y = cw_ref[2:3, cols] * u + pltpu.roll(y, 1, 0)
        return cb_ref[:, cols] + y[BF16_ROWS:, :]

    for c in range(tn // sub):
        cols = slice(c * sub, (c + 1) * sub)
        gate = conv(wg_ref, cwg_ref, cbg_ref, cols)
        val = conv(wv_ref, cwv_ref, cbv_ref, cols)
        o_ref[:, cols] = (gate * jax.nn.sigmoid(gate) * val).astype(o_ref.dtype)


def _ffn_up(h2, w_up, conv_w, conv_b, seq):
    t, d = h2.shape
    tm, tn, sub = 1024, 512, 256
    tpb = seq // tm
    nj = D_FF // tn
    hb = tm // BF16_ROWS
    return pl.pallas_call(
        functools.partial(_ffn_up_kernel, tm=tm, tn=tn, sub=sub, tpb=tpb),
        grid=(nj, t // tm),
        in_specs=[
            pl.BlockSpec((tm, d), lambda j, i: (i, 0)),
            pl.BlockSpec((BF16_ROWS, d), lambda j, i: (jnp.maximum(i * hb - 1, 0), 0)),
            pl.BlockSpec((d, tn), lambda j, i: (0, j)),
            pl.BlockSpec((d, tn), lambda j, i: (0, j + nj)),
            pl.BlockSpec((CONV_WIDTH, tn), lambda j, i: (0, j)),
            pl.BlockSpec((CONV_WIDTH, tn), lambda j, i: (0, j + nj)),
            pl.BlockSpec((1, tn), lambda j, i: (0, j)),
            pl.BlockSpec((1, tn), lambda j, i: (0, j + nj)),
        ],
        out_specs=pl.BlockSpec((tm, tn), lambda j, i: (i, j)),
        out_shape=jax.ShapeDtypeStruct((t, D_FF), BF16),
        scratch_shapes=[pltpu.VMEM((tm + BF16_ROWS, d), BF16),
                        pltpu.VMEM((d, tn), BF16), pltpu.VMEM((d, tn), BF16)],
        compiler_params=_cparams(("arbitrary", "arbitrary")),
        name="ffn_up",
    )(h2, h2, w_up, w_up, conv_w, conv_w, conv_b, conv_b)


def _ffn_down_kernel(a_ref, w_ref, x_ref, g_ref, mod_ref, o_ref):
    y = jnp.dot(a_ref[...], w_ref[...], preferred_element_type=F32)
    o_ref[...] = x_ref[...] + mod_ref[GT2:GT2 + 1, :] * _rms(y, g_ref[...])


def _ffn_down(a, w, x1, g, mod, seq):
    t, d = x1.shape
    tm = 256
    tpb = seq // tm
    return pl.pallas_call(
        _ffn_down_kernel,
        grid=(t // tm,),
        in_specs=[
            pl.BlockSpec((tm, D_FF), lambda i: (i, 0)),
            pl.BlockSpec((D_FF, d), lambda i: (0, 0), pipeline_mode=pl.Buffered(1)),
            pl.BlockSpec((tm, d), lambda i: (i, 0)),
            pl.BlockSpec((1, d), lambda i: (0, 0)),
            pl.BlockSpec((None, 6, d), lambda i: (i // tpb, 0, 0)),
        ],
        out_specs=pl.BlockSpec((tm, d), lambda i: (i, 0)),
        out_shape=jax.ShapeDtypeStruct((t, d), F32),
        compiler_params=_cparams(("parallel",)),
        name="ffn_down",
    )(a, w, x1, g, mod)


def _rot_cols(w):
    half = MLA_ROPE // 2
    w = w.reshape(w.shape[0], -1, MLA_ROPE)
    return jnp.concatenate([-w[..., half:], w[..., :half]], axis=-1).reshape(w.shape[0], -1)


def _prep_w_uq(w_uq):
    w = (w_uq * (MLA_QK ** -0.5 * LOG2E)).reshape(MLA_Q_RANK, MLA_HEADS, MLA_QK)
    wn = w[..., :MLA_NOPE].reshape(MLA_Q_RANK, -1)
    wr = w[..., MLA_NOPE:].reshape(MLA_Q_RANK, -1)
    return wn.astype(BF16), wr.astype(BF16), _rot_cols(wr).astype(BF16)


def _prep_w_ukv(w_ukv):
    w = w_ukv.reshape(MLA_KV_RANK, MLA_HEADS, MLA_NOPE + MLA_V)
    wk = w[..., :MLA_NOPE].reshape(MLA_KV_RANK, -1)
    wvt = w[..., MLA_NOPE:].reshape(MLA_KV_RANK, -1).T
    return wk.astype(BF16), wvt.astype(BF16)


def _rope_tables2(seq):
    pos = jnp.arange(seq, dtype=F32)
    inv = ROPE_THETA ** (-jnp.arange(0, MLA_ROPE, 2, dtype=F32) / MLA_ROPE)
    ang = pos[:, None] * inv[None, :]
    ang = jnp.concatenate([ang, ang, ang, ang], axis=-1)
    return jnp.cos(ang), jnp.sin(ang)


def kernel(x, c, w_ada, b_ada, g_pre_mix, g_post_mix, w_in, g_q_lat, w_uq, g_kv_lat, w_ukv,
           rel_bias, sinks, w_o, g_pre_ffn, g_post_ffn, w_up, conv_w, conv_b, w_down):
    batch, seq, d = x.shape
    depth = w_ada.shape[0]
    xt = x.reshape(batch * seq, d)
    c8 = jnp.pad(c, ((0, 8 - batch), (0, 0)))
    cos2, sin2 = _rope_tables2(seq)
    bias = _t5bias(rel_bias)
    for l in range(depth):
        mod = _ada(c8, w_ada[l], b_ada[l][None, :])[:batch].reshape(batch, 6, d)
        z = _inproj(_prenorm(xt, g_pre_mix[l][None, :], mod, seq), w_in[l].T)
        wn, wr, wrr = _prep_w_uq(w_uq[l])
        qn, qr = _qproj(z, g_q_lat[l][None, :], wn, wr, wrr, cos2, sin2, seq)
        wk, wvt = _prep_w_ukv(w_ukv[l])
        kn, vt, kr2 = _kvproj(z, g_kv_lat[l][None, :], wk, wvt, cos2, sin2, batch, seq)
        o_a = _mla(qn, qr, kn, kr2, vt, batch, seq)
        o_b = _swa(z, bias, sinks[l], batch, seq)
        x1, h2 = _mix(o_a, o_b, z, w_o[l].astype(BF16), xt, g_post_mix[l][None, :],
                      g_pre_ffn[l][None, :], mod, seq)
        a = _ffn_up(h2, w_up[l], conv_w[l], conv_b[l][None, :], seq)
        xt = _ffn_down(a, w_down[l].astype(BF16), x1, g_post_ffn[l][None, :], mod, seq)
    return xt.reshape(batch, seq, d)
```

```python
import functools
import math

import jax
import jax.numpy as jnp
from jax import lax
from jax.experimental import pallas as pl
from jax.experimental.pallas import tpu as pltpu

F32 = jnp.float32
BF16 = jnp.bfloat16

D_MODEL = 2048
MLA_NOPE = 128
MLA_ROPE = 64
MLA_V = 128
MLA_HEADS = D_MODEL // MLA_V
MLA_Q_RANK = 768
MLA_KV_RANK = 512
MLA_QK = MLA_NOPE + MLA_ROPE
ROPE_THETA = 10000.0
SWA_HEAD_DIM = 64
SWA_HEADS = D_MODEL // SWA_HEAD_DIM
SWA_KV_HEADS = 4
SWA_GROUP = SWA_HEADS // SWA_KV_HEADS
SWA_KV = SWA_KV_HEADS * SWA_HEAD_DIM
WINDOW = 128
BLOCK = 128
REL_BUCKETS = 32
REL_MAX_DIST = 128
D_FF = 5632
CONV_WIDTH = 3
EPS = 1e-6
NEG = -1e30
LOG2E = 1.4426950408889634

VMEM_LIMIT_BYTES = 56 * 1024 * 1024
LANES = 128
BF16_ROWS = 16
MLA_TILE = 512
MLA_VX = MLA_V + BF16_ROWS

Z_CQ = 0
Z_CKV = 768
Z_KR = 1280
Z_KRR = 1408
Z_QS = 1536
Z_KS = 3584
Z_VS = 3840
Z_GA = 4096
Z_GB = 6144
Z_COLS = 8192
W_IN_KR = MLA_Q_RANK + MLA_KV_RANK
W_IN_QS = W_IN_KR + MLA_ROPE

SH1, SC1, GT1, SH2, SC2, GT2 = range(6)


def _cparams(sem):
    return pltpu.CompilerParams(dimension_semantics=sem, vmem_limit_bytes=VMEM_LIMIT_BYTES)


def _rms(x, g):
    ms = jnp.mean(x * x, axis=-1, keepdims=True)
    return x * lax.rsqrt(ms + EPS) * g


def _ada_kernel(c_ref, w_ref, b_ref, o_ref):
    c = c_ref[...]
    ca = (c * jax.nn.sigmoid(c)).astype(BF16)
    o_ref[...] = jnp.dot(ca, w_ref[...].astype(BF16), preferred_element_type=F32) + b_ref[...]


def _ada(c8, w_ada, b_ada):
    n = w_ada.shape[1]
    tn = 1024
    return pl.pallas_call(
        _ada_kernel,
        grid=(n // tn,),
        in_specs=[
            pl.BlockSpec((8, D_MODEL), lambda j: (0, 0)),
            pl.BlockSpec((D_MODEL, tn), lambda j: (0, j)),
            pl.BlockSpec((1, tn), lambda j: (0, j)),
        ],
        out_specs=pl.BlockSpec((8, tn), lambda j: (0, j)),
        out_shape=jax.ShapeDtypeStruct((8, n), F32),
        compiler_params=_cparams(("arbitrary",)),
        name="ada",
    )(c8, w_ada, b_ada)


def _fill_normed(x_ref, g, sc, sh, h_ref, row0, nrows, chunk):
    def body(i, carry):
        r = pl.multiple_of(i * chunk, chunk)
        y = _rms(x_ref[pl.ds(r, chunk), :], g)
        h_ref[pl.ds(row0 + r, chunk), :] = (y * (1.0 + sc) + sh).astype(BF16)
        return carry

    lax.fori_loop(0, nrows // chunk, body, 0)


def _prenorm_kernel(x_ref, g_ref, mod_ref, o_ref, *, tm):
    _fill_normed(x_ref, g_ref[...], mod_ref[SC1:SC1 + 1, :], mod_ref[SH1:SH1 + 1, :],
                 o_ref, 0, tm, 128)


def _prenorm(x2, g, mod, seq):
    t, d = x2.shape
    tm = 1024
    tpb = seq // tm
    return pl.pallas_call(
        functools.partial(_prenorm_kernel, tm=tm),
        grid=(t // tm,),
        in_specs=[
            pl.BlockSpec((tm, d), lambda i: (i, 0)),
            pl.BlockSpec((1, d), lambda i: (0, 0)),
            pl.BlockSpec((None, 6, d), lambda i: (i // tpb, 0, 0)),
        ],
        out_specs=pl.BlockSpec((tm, d), lambda i: (i, 0)),
        out_shape=jax.ShapeDtypeStruct((t, d), BF16),
        compiler_params=_cparams(("parallel",)),
        name="prenorm",
    )(x2, g, mod)


INPROJ_TN = 512
KR_TILE = Z_KR // INPROJ_TN
QS_TILES = (Z_QS // INPROJ_TN, Z_KS // INPROJ_TN)


def _inproj_kernel(h_ref, *refs, nw):
    wt_refs, o_ref, wb_ref = refs[:nw], refs[nw], refs[nw + 1]
    j = pl.program_id(0)

    @pl.when(pl.program_id(1) == 0)
    def _():
        for w, wt_ref in enumerate(wt_refs):
            win = j * nw + w
            rows = slice(w * INPROJ_TN, (w + 1) * INPROJ_TN)
            is_qs = (win >= QS_TILES[0]) & (win < QS_TILES[1])
            scale = jnp.where(is_qs, SWA_HEAD_DIM ** -0.5 * LOG2E, 1.0).astype(F32)
            wb_ref[rows, :] = (wt_ref[...] * scale).astype(BF16)
            if w != KR_TILE % nw:
                continue

            @pl.when(win == KR_TILE)
            def _():
                lo = Z_KR - KR_TILE * INPROJ_TN
                kr = wt_ref[lo:lo + MLA_ROPE, :]
                half = MLA_ROPE // 2
                rot = jnp.concatenate([-kr[half:, :], kr[:half, :]], axis=0)
                wb_ref[w * INPROJ_TN + lo:(w + 1) * INPROJ_TN, :] = jnp.concatenate(
                    [kr, kr, rot, rot], axis=0).astype(BF16)

    o_ref[...] = lax.dot_general(h_ref[...], wb_ref[...], (((1,), (1,)), ((), ())),
                                 preferred_element_type=F32).astype(o_ref.dtype)


def _inproj(h, wt):
    t, d = h.shape
    tm, nw = 1024, 2
    tn = nw * INPROJ_TN

    def src_row(win):
        u = MLA_ROPE
        k = INPROJ_TN // u
        return jnp.where(win <= KR_TILE, win * k, W_IN_QS // u + (win - KR_TILE - 1) * k) * u

    def wspec(w):
        return pl.BlockSpec((pl.Element(INPROJ_TN), pl.Element(d)),
                            lambda j, i: (src_row(j * nw + w), 0))

    return pl.pallas_call(
        functools.partial(_inproj_kernel, nw=nw),
        grid=(Z_COLS // tn, t // tm),
        in_specs=[pl.BlockSpec((tm, d), lambda j, i: (i, 0))] + [wspec(w) for w in range(nw)],
        out_specs=pl.BlockSpec((tm, tn), lambda j, i: (i, j)),
        out_shape=jax.ShapeDtypeStruct((t, Z_COLS), BF16),
        scratch_shapes=[pltpu.VMEM((tn, d), BF16)],
        compiler_params=_cparams(("arbitrary", "arbitrary")),
        name="inproj",
    )(h, *([wt] * nw))


def _rope_chunks(a, b, cos, sin):
    n = a.shape[1] // LANES
    out = [a[:, LANES * i:LANES * (i + 1)] * cos + b[:, LANES * i:LANES * (i + 1)] * sin
           for i in range(n)]
    return out[0] if n == 1 else jnp.concatenate(out, axis=1)


def _qproj_kernel(cq_ref, g_ref, wn_ref, wr_ref, wrr_ref, cos_ref, sin_ref, qn_ref, qr_ref):
    cqn = _rms(cq_ref[...].astype(F32), g_ref[...]).astype(BF16)
    qn_ref[...] = jnp.dot(cqn, wn_ref[...], preferred_element_type=F32).astype(BF16)
    a = jnp.dot(cqn, wr_ref[...], preferred_element_type=F32)
    b = jnp.dot(cqn, wrr_ref[...], preferred_element_type=F32)
    qr_ref[...] = _rope_chunks(a, b, cos_ref[...], sin_ref[...]).astype(BF16)


def _qproj(z, g, wn, wr, wrr, cos2, sin2, seq):
    t = z.shape[0]
    tm = 1024
    spb = seq // tm
    hn = MLA_HEADS * MLA_NOPE
    hr = MLA_HEADS * MLA_ROPE
    return pl.pallas_call(
        _qproj_kernel,
        grid=(t // tm,),
        in_specs=[
            pl.BlockSpec((tm, MLA_Q_RANK), lambda i: (i, Z_CQ // MLA_Q_RANK)),
            pl.BlockSpec((1, MLA_Q_RANK), lambda i: (0, 0)),
            pl.BlockSpec((MLA_Q_RANK, hn), lambda i: (0, 0)),
            pl.BlockSpec((MLA_Q_RANK, hr), lambda i: (0, 0)),
            pl.BlockSpec((MLA_Q_RANK, hr), lambda i: (0, 0)),
            pl.BlockSpec((tm, LANES), lambda i: (i % spb, 0)),
            pl.BlockSpec((tm, LANES), lambda i: (i % spb, 0)),
        ],
        out_specs=[
            pl.BlockSpec((tm, hn), lambda i: (i, 0)),
            pl.BlockSpec((tm, hr), lambda i: (i, 0)),
        ],
        out_shape=[jax.ShapeDtypeStruct((t, hn), BF16), jax.ShapeDtypeStruct((t, hr), BF16)],
        compiler_params=_cparams(("parallel",)),
        name="qproj",
    )(z, g, wn, wr, wrr, cos2, sin2)


def _kvproj_kernel(ckv_ref, g_ref, wk_ref, wvt_ref, kr_ref, krr_ref, cos_ref, sin_ref,
                   kn_ref, vt_ref, kro_ref):
    ckvn = _rms(ckv_ref[...].astype(F32), g_ref[...]).astype(BF16)
    kn_ref[...] = jnp.dot(ckvn, wk_ref[...], preferred_element_type=F32).astype(BF16)
    vt = lax.dot_general(wvt_ref[...], ckvn, (((1,), (1,)), ((), ())), preferred_element_type=F32)
    vt_ref[:, :MLA_V, :] = vt.reshape(MLA_HEADS, MLA_V, vt.shape[1]).astype(BF16)
    vt_ref[:, MLA_V:, :] = jnp.ones((MLA_HEADS, MLA_VX - MLA_V, vt.shape[1]), BF16)
    kro_ref[...] = (kr_ref[...].astype(F32) * cos_ref[...]
                    + krr_ref[...].astype(F32) * sin_ref[...]).astype(BF16)


def _kvproj(z, g, wk, wvt, cos2, sin2, batch, seq):
    t = z.shape[0]
    tm = MLA_TILE
    spb = seq // tm
    hn = MLA_HEADS * MLA_NOPE
    hv = MLA_HEADS * MLA_V
    return pl.pallas_call(
        _kvproj_kernel,
        grid=(t // tm,),
        in_specs=[
            pl.BlockSpec((pl.Element(tm), pl.Element(MLA_KV_RANK)), lambda i: (i * tm, Z_CKV)),
            pl.BlockSpec((1, MLA_KV_RANK), lambda i: (0, 0)),
            pl.BlockSpec((MLA_KV_RANK, hn), lambda i: (0, 0)),
            pl.BlockSpec((hv, MLA_KV_RANK), lambda i: (0, 0)),
            pl.BlockSpec((tm, LANES), lambda i: (i, Z_KR // LANES)),
            pl.BlockSpec((tm, LANES), lambda i: (i, Z_KRR // LANES)),
            pl.BlockSpec((tm, LANES), lambda i: (i % spb, 0)),
            pl.BlockSpec((tm, LANES), lambda i: (i % spb, 0)),
        ],
        out_specs=[
            pl.BlockSpec((tm, hn), lambda i: (i, 0)),
            pl.BlockSpec((None, MLA_HEADS, None, MLA_VX, tm), lambda i: (i // spb, 0, i % spb, 0, 0)),
            pl.BlockSpec((tm, LANES), lambda i: (i, 0)),
        ],
        out_shape=[jax.ShapeDtypeStruct((t, hn), BF16),
                   jax.ShapeDtypeStruct((batch, MLA_HEADS, spb, MLA_VX, tm), BF16),
                   jax.ShapeDtypeStruct((t, LANES), BF16)],
        compiler_params=_cparams(("parallel",)),
        name="kvproj",
    )(z, g, wk, wvt, z, z, cos2, sin2)


def _mla_kernel(qn_ref, qr_ref, kn_ref, kr_ref, vt_ref, o_ref, q_ref, s_ref, p_ref, acc_ref,
                *, tile, nq):
    head = pl.program_id(1)
    lo = (head % 2) * MLA_ROPE
    half = tile // 2

    def scores(q, k):
        k0 = k * tile if isinstance(k, int) else pl.multiple_of(k * tile, tile)
        kc = jnp.concatenate([kn_ref[pl.ds(k0, tile), :], kr_ref[pl.ds(k0, tile), :]], axis=1)
        return lax.dot_general(kc, q, (((1,), (1,)), ((), ())), preferred_element_type=F32)

    def softmax(s, m):
        m_new = jnp.maximum(m, jnp.max(s, axis=0, keepdims=True))
        return jnp.exp2(s - m_new).astype(BF16), jnp.exp2(m - m_new), m_new

    def values(p, k):
        return jnp.dot(vt_ref[k], p, preferred_element_type=F32)

    for qi in range(nq):
        rows = slice(qi * tile, (qi + 1) * tile)
        qr = qr_ref[rows, :]
        lane = lax.broadcasted_iota(jnp.int32, qr.shape, 1)
        keep = (lane >= lo) & (lane < lo + MLA_ROPE)
        q_ref[...] = jnp.concatenate(
            [qn_ref[rows, :], jnp.where(keep, qr, jnp.zeros_like(qr))], axis=1)

        def put_scores(k):
            slot = k % 3
            if k < qi:
                s_ref[slot] = scores(q_ref[...], k)
                return
            kc = jnp.concatenate([kn_ref[rows, :], kr_ref[rows, :]], axis=1)
            nt = (((1,), (1,)), ((), ()))
            s_ref[slot, :half, :] = lax.dot_general(kc[:half], q_ref[...], nt,
                                                    preferred_element_type=F32)
            s_ref[slot, half:, half:] = lax.dot_general(kc[half:], q_ref[half:, :], nt,
                                                        preferred_element_type=F32)

        put_scores(0)
        m = jnp.full((1, tile), NEG, F32)
        for k in range(qi):
            put_scores(k + 1)
            p, a, m = softmax(s_ref[k % 3], m)
            p_ref[k % 2] = p
            pv = values(p_ref[k % 2], k)
            acc_ref[...] = pv if k == 0 else a * acc_ref[...] + pv

        slot = qi % 3
        key = lax.broadcasted_iota(jnp.int32, (half, half), 0)
        qry = lax.broadcasted_iota(jnp.int32, (half, half), 1)
        s_tl = jnp.where(key <= qry, s_ref[slot, :half, :half], NEG)
        s_tr = s_ref[slot, :half, half:]
        s_br = jnp.where(key <= qry, s_ref[slot, half:, half:], NEG)
        m_l = jnp.maximum(m[:, :half], jnp.max(s_tl, axis=0, keepdims=True))
        m_r = jnp.maximum(jnp.maximum(m[:, half:], jnp.max(s_tr, axis=0, keepdims=True)),
                          jnp.max(s_br, axis=0, keepdims=True))
        a = jnp.exp2(m - jnp.concatenate([m_l, m_r], axis=1))
        p_top = jnp.concatenate([jnp.exp2(s_tl - m_l), jnp.exp2(s_tr - m_r)], axis=1).astype(BF16)
        p_bot = jnp.exp2(s_br - m_r).astype(BF16)
        vt = vt_ref[qi]
        pv = jnp.dot(vt[:, :half], p_top, preferred_element_type=F32)
        pv_r = jnp.dot(vt[:, half:], p_bot, preferred_element_type=F32)
        acc = pv if qi == 0 else a * acc_ref[...] + pv
        num = jnp.concatenate([acc[:MLA_V, :half], acc[:MLA_V, half:] + pv_r[:MLA_V, :]], axis=1)
        den = jnp.concatenate([acc[MLA_V:MLA_V + 1, :half],
                               acc[MLA_V:MLA_V + 1, half:] + pv_r[MLA_V:MLA_V + 1, :]], axis=1)
        o_ref[rows, :] = (num / den).T.astype(o_ref.dtype)


def _mla(qn, qr, kn, kr2, vt, batch, seq):
    tile = MLA_TILE
    nq = seq // tile
    t = batch * seq
    tok = lambda b, h: (b, h)
    return pl.pallas_call(
        functools.partial(_mla_kernel, tile=tile, nq=nq),
        grid=(batch, MLA_HEADS),
        in_specs=[
            pl.BlockSpec((seq, LANES), tok),
            pl.BlockSpec((seq, LANES), lambda b, h: (b, h // 2)),
            pl.BlockSpec((seq, LANES), tok),
            pl.BlockSpec((seq, LANES), lambda b, h: (b, 0)),
            pl.BlockSpec((None, None, nq, MLA_VX, tile), lambda b, h: (b, h, 0, 0, 0)),
        ],
        out_specs=pl.BlockSpec((seq, LANES), tok),
        out_shape=jax.ShapeDtypeStruct((t, MLA_HEADS * MLA_V), BF16),
        scratch_shapes=[
            pltpu.VMEM((tile, 2 * LANES), BF16),
            pltpu.VMEM((3, tile, tile), F32),
            pltpu.VMEM((2, tile, tile), BF16),
            pltpu.VMEM((MLA_VX, tile), F32),
        ],
        compiler_params=_cparams(("parallel", "arbitrary")),
        name="mla",
    )(qn, qr, kn, kr2, vt)


def _t5_bucket(dist):
    max_exact = REL_BUCKETS // 2
    n = jnp.maximum(dist, 0)
    large = max_exact + (jnp.log(jnp.maximum(n, 1).astype(F32) / max_exact)
                         / math.log(REL_MAX_DIST / max_exact)
                         * (REL_BUCKETS - max_exact)).astype(jnp.int32)
    large = jnp.minimum(large, REL_BUCKETS - 1)
    return jnp.where(n < max_exact, n, large)


def _t5bias_kernel(rb_ref, bucket_ref, o_ref):
    kvh = pl.program_id(0)
    bucket = bucket_ref[...]
    key = lax.broadcasted_iota(jnp.int32, bucket.shape, 0)
    qry = lax.broadcasted_iota(jnp.int32, bucket.shape, 1)
    dist = BLOCK + qry - key
    mask = (dist >= 0) & (dist < WINDOW)
    mask_first = mask & (key >= BLOCK)
    for g in range(SWA_GROUP):
        acc = jnp.zeros(bucket.shape, F32)
        for b in range(REL_BUCKETS):
            acc = jnp.where(bucket == b, rb_ref[b, kvh * SWA_GROUP + g], acc)
        cols = slice(g * BLOCK, (g + 1) * BLOCK)
        o_ref[0, :, cols] = jnp.where(mask, acc * LOG2E, NEG)
        o_ref[1, :, cols] = jnp.where(mask_first, acc * LOG2E, NEG)


def _t5bias(rel_bias):
    a = jnp.arange(BLOCK)
    bidx = jnp.arange(2 * BLOCK)
    bucket = _t5_bucket(BLOCK + a[None, :] - bidx[:, None]).astype(jnp.int32)
    return pl.pallas_call(
        _t5bias_kernel,
        grid=(SWA_KV_HEADS,),
        in_specs=[
            pl.BlockSpec(memory_space=pltpu.SMEM),
            pl.BlockSpec((2 * BLOCK, BLOCK), lambda h: (0, 0)),
        ],
        out_specs=pl.BlockSpec((2, None, 2 * BLOCK, SWA_GROUP * BLOCK), lambda h: (0, h, 0, 0)),
        out_shape=jax.ShapeDtypeStruct((2, SWA_KV_HEADS, 2 * BLOCK, SWA_GROUP * BLOCK), F32),
        compiler_params=_cparams(("arbitrary",)),
        name="t5bias",
    )(rel_bias, bucket)


def _roll_half(x):
    return pltpu.bitcast(pltpu.roll(pltpu.bitcast(x, jnp.uint32), LANES // 2, 1), x.dtype)


def _keep_lanes(x, word_mask):
    return pltpu.bitcast(pltpu.bitcast(x, jnp.uint32) & word_mask, x.dtype)


def _swa_kernel(sink_ref, q_ref, kp_ref, kc_ref, vp_ref, vc_ref, bias_ref, o_ref, *, nqb):
    seq_start = jnp.where(pl.program_id(1) == 0, 1, 0)
    k_all = jnp.concatenate([kp_ref[...], kc_ref[...]], axis=0)
    v_all = jnp.concatenate([vp_ref[...], vc_ref[...]], axis=0)
    vt_all = v_all.astype(F32).T.astype(BF16)
    low = lax.broadcasted_iota(jnp.int32, (BLOCK, LANES), 1) < SWA_HEAD_DIM
    low2 = lax.broadcasted_iota(jnp.int32, (2 * BLOCK, LANES), 1) < SWA_HEAD_DIM
    ones = jnp.ones((BF16_ROWS, 2 * BLOCK), BF16)
    lane = lax.broadcasted_iota(jnp.int32, (BLOCK // 2, LANES), 1)
    keep_low = jnp.where(lane < SWA_HEAD_DIM, jnp.uint32(0xFFFFFFFF), jnp.uint32(0))
    keep_high = ~keep_low
    for qb in range(nqb):
        first = seq_start if qb == 0 else 0
        q = q_ref[qb * BLOCK:(qb + 1) * BLOCK, :]
        k = k_all[qb * BLOCK:(qb + 2) * BLOCK, :]
        vt = vt_all[:, qb * BLOCK:(qb + 2) * BLOCK]
        outs = []
        for kvh in range(SWA_KV_HEADS):
            pair = slice(LANES * (kvh // 2), LANES * (kvh // 2 + 1))
            kb = k[:, pair]
            kr = _roll_half(kb)
            kd = jnp.where(low2, kr, kb) if kvh % 2 else jnp.where(low2, kb, kr)
            vth = vt[SWA_HEAD_DIM * kvh:SWA_HEAD_DIM * (kvh + 1), :]
            vtx = jnp.concatenate([vth, vth, ones], axis=0)
            qs, sinks = [], []
            for g in range(SWA_GROUP):
                hq = kvh * SWA_GROUP + g
                blk = q[:, LANES * (hq // 2):LANES * (hq // 2 + 1)]
                qs.append(_keep_lanes(blk, keep_low if hq % 2 == 0 else keep_high))
                sinks.append(jnp.full((1, BLOCK), sink_ref[hq] * LOG2E, F32))
            qg = jnp.concatenate(qs, axis=0)
            sink = jnp.concatenate(sinks, axis=1)
            s = lax.dot_general(kd, qg, (((1,), (1,)), ((), ())), preferred_element_type=F32)
            s = s + bias_ref[first, kvh]
            m = jnp.maximum(jnp.max(s, axis=0, keepdims=True), sink)
            e = jnp.exp2(s - m).astype(BF16)
            ox = jnp.dot(vtx, e, preferred_element_type=F32)
            denom = ox[2 * SWA_HEAD_DIM:2 * SWA_HEAD_DIM + 1, :] + jnp.exp2(sink - m)
            o = ox[:2 * SWA_HEAD_DIM, :] / denom
            for j in range(SWA_GROUP // 2):
                a = o[:, 2 * j * BLOCK:(2 * j + 1) * BLOCK].T
                b = o[:, (2 * j + 1) * BLOCK:(2 * j + 2) * BLOCK].T
                outs.append(jnp.where(low, a, b).astype(BF16))
        o_ref[qb * BLOCK:(qb + 1) * BLOCK, :] = jnp.concatenate(outs, axis=1)


def _swa(z, bias, sinks, batch, seq):
    nqb = 4
    rows = nqb * BLOCK
    ns = seq // rows
    t = batch * seq
    qcols = SWA_HEADS * SWA_HEAD_DIM

    def cur(col):
        return pl.BlockSpec((rows, SWA_KV), lambda b, n: (b * ns + n, col))

    def prev(col):
        return pl.BlockSpec((BLOCK, SWA_KV),
                            lambda b, n: ((b * ns + n) * nqb - jnp.minimum(n, 1), col))

    return pl.pallas_call(
        functools.partial(_swa_kernel, nqb=nqb),
        grid=(batch, ns),
        in_specs=[
            pl.BlockSpec(memory_space=pltpu.SMEM),
            pl.BlockSpec((pl.Element(rows), pl.Element(qcols)),
                         lambda b, n: ((b * ns + n) * rows, Z_QS)),
            prev(Z_KS // SWA_KV), cur(Z_KS // SWA_KV),
            prev(Z_VS // SWA_KV), cur(Z_VS // SWA_KV),
            pl.BlockSpec(bias.shape, lambda b, n: (0, 0, 0, 0), pipeline_mode=pl.Buffered(1)),
        ],
        out_specs=pl.BlockSpec((rows, qcols), lambda b, n: (b * ns + n, 0)),
        out_shape=jax.ShapeDtypeStruct((t, qcols), BF16),
        compiler_params=_cparams(("parallel", "arbitrary")),
        name="swa",
    )(sinks, z, z, z, z, z, bias)


def _mix_kernel(oa_ref, ob_ref, ga_ref, gb_ref, w_ref, x_ref, g_ref, g2_ref, mod_ref, o_ref, h2_ref,
                *, tm, sub):
    for r in range(tm // sub):
        rows = slice(r * sub, (r + 1) * sub)
        ga = jax.nn.sigmoid(ga_ref[rows, :].astype(F32))
        gb = jax.nn.sigmoid(gb_ref[rows, :].astype(F32))
        u = ga * oa_ref[rows, :].astype(F32) + gb * ob_ref[rows, :].astype(F32)
        mix = jnp.dot(u.astype(BF16), w_ref[...], preferred_element_type=F32)
        x1 = x_ref[rows, :] + mod_ref[GT1:GT1 + 1, :] * _rms(mix, g_ref[...])
        o_ref[rows, :] = x1
        h2 = _rms(x1, g2_ref[...]) * (1.0 + mod_ref[SC2:SC2 + 1, :]) + mod_ref[SH2:SH2 + 1, :]
        h2_ref[rows, :] = h2.astype(BF16)


def _mix(oa, ob, z, w, x2, g, g2, mod, seq):
    t, d = x2.shape
    tm, sub = 512, 256
    tpb = seq // tm
    row = lambda i: (i, 0)
    return pl.pallas_call(
        functools.partial(_mix_kernel, tm=tm, sub=sub),
        grid=(t // tm,),
        in_specs=[
            pl.BlockSpec((tm, d), row),
            pl.BlockSpec((tm, d), row),
            pl.BlockSpec((tm, d), lambda i: (i, Z_GA // d)),
            pl.BlockSpec((tm, d), lambda i: (i, Z_GB // d)),
            pl.BlockSpec((d, d), lambda i: (0, 0), pipeline_mode=pl.Buffered(1)),
            pl.BlockSpec((tm, d), row),
            pl.BlockSpec((1, d), lambda i: (0, 0)),
            pl.BlockSpec((1, d), lambda i: (0, 0)),
            pl.BlockSpec((None, 6, d), lambda i: (i // tpb, 0, 0)),
        ],
        out_specs=[pl.BlockSpec((tm, d), row), pl.BlockSpec((tm, d), row)],
        out_shape=[jax.ShapeDtypeStruct((t, d), F32), jax.ShapeDtypeStruct((t, d), BF16)],
        compiler_params=_cparams(("parallel",)),
        name="mix",
    )(oa, ob, z, z, w, x2, g, g2, mod)


def _ffn_up_kernel(h2_ref, halo_ref, wg32_ref, wv32_ref, cwg_ref, cwv_ref, cbg_ref, cbv_ref,
                   o_ref, h_ref, wg_ref, wv_ref, *, tm, tn, sub, tpb):
    i = pl.program_id(1)

    @pl.when(i == 0)
    def _():
        wg_ref[...] = wg32_ref[...].astype(BF16)
        wv_ref[...] = wv32_ref[...].astype(BF16)

    halo = halo_ref[...]
    h_ref[0:BF16_ROWS, :] = jnp.where(i % tpb == 0, jnp.zeros_like(halo), halo)
    h_ref[BF16_ROWS:, :] = h2_ref[...]
    h = h_ref[...]

    def conv(w_ref, cw_ref, cb_ref, cols):
        u = jnp.dot(h, w_ref[:, cols], preferred_element_type=F32)
        y = cw_ref[1:2, cols] * u + pltpu.roll(cw_ref[0:1, cols] * u, 1, 0)
        y = cw_ref[2:3, cols] * u + pltpu.roll(y, 1, 0)
        return cb_ref[:, cols] + y[BF16_ROWS:, :]

    for c in range(tn // sub):
        cols = slice(c * sub, (c + 1) * sub)
        gate = conv(wg_ref, cwg_ref, cbg_ref, cols)
        val = conv(wv_ref, cwv_ref, cbv_ref, cols)
        o_ref[:, cols] = (gate * jax.nn.sigmoid(gate) * val).astype(o_ref.dtype)


def _ffn_up(h2, w_up, conv_w, conv_b, seq):
    t, d = h2.shape
    tm, tn, sub = 1024, 512, 256
    tpb = seq // tm
    nj = D_FF // tn
    hb = tm // BF16_ROWS
    return pl.pallas_call(
        functools.partial(_ffn_up_kernel, tm=tm, tn=tn, sub=sub, tpb=tpb),
        grid=(nj, t // tm),
        in_specs=[
            pl.BlockSpec((tm, d), lambda j, i: (i, 0)),
            pl.BlockSpec((BF16_ROWS, d), lambda j, i: (jnp.maximum(i * hb - 1, 0), 0)),
            pl.BlockSpec((d, tn), lambda j, i: (0, j)),
            pl.BlockSpec((d, tn), lambda j, i: (0, j + nj)),
            pl.BlockSpec((CONV_WIDTH, tn), lambda j, i: (0, j)),
            pl.BlockSpec((CONV_WIDTH, tn), lambda j, i: (0, j + nj)),
            pl.BlockSpec((1, tn), lambda j, i: (0, j)),
            pl.BlockSpec((1, tn), lambda j, i: (0, j + nj)),
        ],
        out_specs=pl.BlockSpec((tm, tn), lambda j, i: (i, j)),
        out_shape=jax.ShapeDtypeStruct((t, D_FF), BF16),
        scratch_shapes=[pltpu.VMEM((tm + BF16_ROWS, d), BF16),
                        pltpu.VMEM((d, tn), BF16), pltpu.VMEM((d, tn), BF16)],
        compiler_params=_cparams(("arbitrary", "arbitrary")),
        name="ffn_up",
    )(h2, h2, w_up, w_up, conv_w, conv_w, conv_b, conv_b)


def _ffn_down_kernel(a_ref, w_ref, x_ref, g_ref, mod_ref, o_ref):
    y = jnp.dot(a_ref[...], w_ref[...], preferred_element_type=F32)
    o_ref[...] = x_ref[...] + mod_ref[GT2:GT2 + 1, :] * _rms(y, g_ref[...])


def _ffn_down(a, w, x1, g, mod, seq):
    t, d = x1.shape
    tm = 256
    tpb = seq // tm
    return pl.pallas_call(
        _ffn_down_kernel,
        grid=(t // tm,),
        in_specs=[
            pl.BlockSpec((tm, D_FF), lambda i: (i, 0)),
            pl.BlockSpec((D_FF, d), lambda i: (0, 0), pipeline_mode=pl.Buffered(1)),
            pl.BlockSpec((tm, d), lambda i: (i, 0)),
            pl.BlockSpec((1, d), lambda i: (0, 0)),
            pl.BlockSpec((None, 6, d), lambda i: (i // tpb, 0, 0)),
        ],
        out_specs=pl.BlockSpec((tm, d), lambda i: (i, 0)),
        out_shape=jax.ShapeDtypeStruct((t, d), F32),
        compiler_params=_cparams(("parallel",)),
        name="ffn_down",
    )(a, w, x1, g, mod)


def _rot_cols(w):
    half = MLA_ROPE // 2
    w = w.reshape(w.shape[0], -1, MLA_ROPE)
    return jnp.concatenate([-w[..., half:], w[..., :half]], axis=-1).reshape(w.shape[0], -1)


def _prep_w_uq(w_uq):
    w = (w_uq * (MLA_QK ** -0.5 * LOG2E)).reshape(MLA_Q_RANK, MLA_HEADS, MLA_QK)
    wn = w[..., :MLA_NOPE].reshape(MLA_Q_RANK, -1)
    wr = w[..., MLA_NOPE:].reshape(MLA_Q_RANK, -1)
    return wn.astype(BF16), wr.astype(BF16), _rot_cols(wr).astype(BF16)


def _prep_w_ukv(w_ukv):
    w = w_ukv.reshape(MLA_KV_RANK, MLA_HEADS, MLA_NOPE + MLA_V)
    wk = w[..., :MLA_NOPE].reshape(MLA_KV_RANK, -1)
    wvt = w[..., MLA_NOPE:].reshape(MLA_KV_RANK, -1).T
    return wk.astype(BF16), wvt.astype(BF16)


def _rope_tables2(seq):
    pos = jnp.arange(seq, dtype=F32)
    inv = ROPE_THETA ** (-jnp.arange(0, MLA_ROPE, 2, dtype=F32) / MLA_ROPE)
    ang = pos[:, None] * inv[None, :]
    ang = jnp.concatenate([ang, ang, ang, ang], axis=-1)
    return jnp.cos(ang), jnp.sin(ang)


def kernel(x, c, w_ada, b_ada, g_pre_mix, g_post_mix, w_in, g_q_lat, w_uq, g_kv_lat, w_ukv,
           rel_bias, sinks, w_o, g_pre_ffn, g_post_ffn, w_up, conv_w, conv_b, w_down):
    batch, seq, d = x.shape
    depth = w_ada.shape[0]
    xt = x.reshape(batch * seq, d)
    c8 = jnp.pad(c, ((0, 8 - batch), (0, 0)))
    cos2, sin2 = _rope_tables2(seq)
    bias = _t5bias(rel_bias)
    for l in range(depth):
        mod = _ada(c8, w_ada[l], b_ada[l][None, :])[:batch].reshape(batch, 6, d)
        z = _inproj(_prenorm(xt, g_pre_mix[l][None, :], mod, seq), w_in[l].T)
        wn, wr, wrr = _prep_w_uq(w_uq[l])
        qn, qr = _qproj(z, g_q_lat[l][None, :], wn, wr, wrr, cos2, sin2, seq)
        wk, wvt = _prep_w_ukv(w_ukv[l])
        kn, vt, kr2 = _kvproj(z, g_kv_lat[l][None, :], wk, wvt, cos2, sin2, batch, seq)
        o_a = _mla(qn, qr, kn, kr2, vt, batch, seq)
        o_b = _swa(z, bias, sinks[l], batch, seq)
        x1, h2 = _mix(o_a, o_b, z, w_o[l].astype(BF16), xt, g_post_mix[l][None, :],
                      g_pre_ffn[l][None, :], mod, seq)
        a = _ffn_up(h2, w_up[l], conv_w[l], conv_b[l][None, :], seq)
        xt = _ffn_down(a, w_down[l].astype(BF16), x1, g_post_ffn[l][None, :], mod, seq)
    return xt.reshape(batch, seq, d)
```

```python
import functools
import math

import jax
import jax.numpy as jnp
from jax import lax
from jax.experimental import pallas as pl
from jax.experimental.pallas import tpu as pltpu

F32 = jnp.float32
BF16 = jnp.bfloat16

D_MODEL = 2048
MLA_NOPE = 128
MLA_ROPE = 64
MLA_V = 128
MLA_HEADS = D_MODEL // MLA_V
MLA_Q_RANK = 768
MLA_KV_RANK = 512
MLA_QK = MLA_NOPE + MLA_ROPE
ROPE_THETA = 10000.0
SWA_HEAD_DIM = 64
SWA_HEADS = D_MODEL // SWA_HEAD_DIM
SWA_KV_HEADS = 4
SWA_GROUP = SWA_HEADS // SWA_KV_HEADS
SWA_KV = SWA_KV_HEADS * SWA_HEAD_DIM
WINDOW = 128
BLOCK = 128
REL_BUCKETS = 32
REL_MAX_DIST = 128
D_FF = 5632
CONV_WIDTH = 3
EPS = 1e-6
NEG = -1e30
LOG2E = 1.4426950408889634

VMEM_LIMIT_BYTES = 56 * 1024 * 1024
LANES = 128
BF16_ROWS = 16
MLA_TILE = 512
MLA_VX = MLA_V + BF16_ROWS

Z_CQ = 0
Z_CKV = 768
Z_KR = 1280
Z_KRR = 1408
Z_QS = 1536
Z_KS = 3584
Z_VS = 3840
Z_GA = 4096
Z_GB = 6144
Z_COLS = 8192
W_IN_KR = MLA_Q_RANK + MLA_KV_RANK
W_IN_QS = W_IN_KR + MLA_ROPE

SH1, SC1, GT1, SH2, SC2, GT2 = range(6)


def _cparams(sem):
    return pltpu.CompilerParams(dimension_semantics=sem, vmem_limit_bytes=VMEM_LIMIT_BYTES)


def _rms(x, g):
    ms = jnp.mean(x * x, axis=-1, keepdims=True)
    return x * lax.rsqrt(ms + EPS) * g


def _ada_kernel(c_ref, w_ref, b_ref, o_ref):
    c = c_ref[...]
    ca = (c * jax.nn.sigmoid(c)).astype(BF16)
    o_ref[...] = jnp.dot(ca, w_ref[...].astype(BF16), preferred_element_type=F32) + b_ref[...]


def _ada(c8, w_ada, b_ada):
    n = w_ada.shape[1]
    tn = 1024
    return pl.pallas_call(
        _ada_kernel,
        grid=(n // tn,),
        in_specs=[
            pl.BlockSpec((8, D_MODEL), lambda j: (0, 0)),
            pl.BlockSpec((D_MODEL, tn), lambda j: (0, j)),
            pl.BlockSpec((1, tn), lambda j: (0, j)),
        ],
        out_specs=pl.BlockSpec((8, tn), lambda j: (0, j)),
        out_shape=jax.ShapeDtypeStruct((8, n), F32),
        compiler_params=_cparams(("arbitrary",)),
        name="ada",
    )(c8, w_ada, b_ada)


def _fill_normed(x_ref, g, sc, sh, h_ref, row0, nrows, chunk):
    def body(i, carry):
        r = pl.multiple_of(i * chunk, chunk)
        y = _rms(x_ref[pl.ds(r, chunk), :], g)
        h_ref[pl.ds(row0 + r, chunk), :] = (y * (1.0 + sc) + sh).astype(BF16)
        return carry

    lax.fori_loop(0, nrows // chunk, body, 0)


def _prenorm_kernel(x_ref, g_ref, mod_ref, o_ref, *, tm):
    _fill_normed(x_ref, g_ref[...], mod_ref[SC1:SC1 + 1, :], mod_ref[SH1:SH1 + 1, :],
                 o_ref, 0, tm, 128)


def _prenorm(x2, g, mod, seq):
    t, d = x2.shape
    tm = 1024
    tpb = seq // tm
    return pl.pallas_call(
        functools.partial(_prenorm_kernel, tm=tm),
        grid=(t // tm,),
        in_specs=[
            pl.BlockSpec((tm, d), lambda i: (i, 0)),
            pl.BlockSpec((1, d), lambda i: (0, 0)),
            pl.BlockSpec((None, 6, d), lambda i: (i // tpb, 0, 0)),
        ],
        out_specs=pl.BlockSpec((tm, d), lambda i: (i, 0)),
        out_shape=jax.ShapeDtypeStruct((t, d), BF16),
        compiler_params=_cparams(("parallel",)),
        name="prenorm",
    )(x2, g, mod)


INPROJ_TN = 512
KR_TILE = Z_KR // INPROJ_TN
QS_TILES = (Z_QS // INPROJ_TN, Z_KS // INPROJ_TN)


def _inproj_kernel(h_ref, *refs, nw):
    wt_refs, o_ref, wb_ref = refs[:nw], refs[nw], refs[nw + 1]
    j = pl.program_id(0)

    @pl.when(pl.program_id(1) == 0)
    def _():
        for w, wt_ref in enumerate(wt_refs):
            win = j * nw + w
            rows = slice(w * INPROJ_TN, (w + 1) * INPROJ_TN)
            is_qs = (win >= QS_TILES[0]) & (win < QS_TILES[1])
            scale = jnp.where(is_qs, SWA_HEAD_DIM ** -0.5 * LOG2E, 1.0).astype(F32)
            wb_ref[rows, :] = (wt_ref[...] * scale).astype(BF16)
            if w != KR_TILE % nw:
                continue

            @pl.when(win == KR_TILE)
            def _():
                lo = Z_KR - KR_TILE * INPROJ_TN
                kr = wt_ref[lo:lo + MLA_ROPE, :]
                half = MLA_ROPE // 2
                rot = jnp.concatenate([-kr[half:, :], kr[:half, :]], axis=0)
                wb_ref[w * INPROJ_TN + lo:(w + 1) * INPROJ_TN, :] = jnp.concatenate(
                    [kr, kr, rot, rot], axis=0).astype(BF16)

    o_ref[...] = lax.dot_general(h_ref[...], wb_ref[...], (((1,), (1,)), ((), ())),
                                 preferred_element_type=F32).astype(o_ref.dtype)


def _inproj(h, wt):
    t, d = h.shape
    tm, nw = 1024, 2
    tn = nw * INPROJ_TN

    def src_row(win):
        u = MLA_ROPE
        k = INPROJ_TN // u
        return jnp.where(win <= KR_TILE, win * k, W_IN_QS // u + (win - KR_TILE - 1) * k) * u

    def wspec(w):
        return pl.BlockSpec((pl.Element(INPROJ_TN), pl.Element(d)),
                            lambda j, i: (src_row(j * nw + w), 0))

    return pl.pallas_call(
        functools.partial(_inproj_kernel, nw=nw),
        grid=(Z_COLS // tn, t // tm),
        in_specs=[pl.BlockSpec((tm, d), lambda j, i: (i, 0))] + [wspec(w) for w in range(nw)],
        out_specs=pl.BlockSpec((tm, tn), lambda j, i: (i, j)),
        out_shape=jax.ShapeDtypeStruct((t, Z_COLS), BF16),
        scratch_shapes=[pltpu.VMEM((tn, d), BF16)],
        compiler_params=_cparams(("arbitrary", "arbitrary")),
        name="inproj",
    )(h, *([wt] * nw))


def _rope_chunks(a, b, cos, sin):
    n = a.shape[1] // LANES
    out = [a[:, LANES * i:LANES * (i + 1)] * cos + b[:, LANES * i:LANES * (i + 1)] * sin
           for i in range(n)]
    return out[0] if n == 1 else jnp.concatenate(out, axis=1)


def _qproj_kernel(cq_ref, g_ref, wn_ref, wr_ref, wrr_ref, cos_ref, sin_ref, qn_ref, qr_ref):
    cqn = _rms(cq_ref[...].astype(F32), g_ref[...]).astype(BF16)
    qn_ref[...] = jnp.dot(cqn, wn_ref[...], preferred_element_type=F32).astype(BF16)
    a = jnp.dot(cqn, wr_ref[...], preferred_element_type=F32)
    b = jnp.dot(cqn, wrr_ref[...], preferred_element_type=F32)
    qr_ref[...] = _rope_chunks(a, b, cos_ref[...], sin_ref[...]).astype(BF16)


def _qproj(z, g, wn, wr, wrr, cos2, sin2, seq):
    t = z.shape[0]
    tm = 1024
    spb = seq // tm
    hn = MLA_HEADS * MLA_NOPE
    hr = MLA_HEADS * MLA_ROPE
    return pl.pallas_call(
        _qproj_kernel,
        grid=(t // tm,),
        in_specs=[
            pl.BlockSpec((tm, MLA_Q_RANK), lambda i: (i, Z_CQ // MLA_Q_RANK)),
            pl.BlockSpec((1, MLA_Q_RANK), lambda i: (0, 0)),
            pl.BlockSpec((MLA_Q_RANK, hn), lambda i: (0, 0)),
            pl.BlockSpec((MLA_Q_RANK, hr), lambda i: (0, 0)),
            pl.BlockSpec((MLA_Q_RANK, hr), lambda i: (0, 0)),
            pl.BlockSpec((tm, LANES), lambda i: (i % spb, 0)),
            pl.BlockSpec((tm, LANES), lambda i: (i % spb, 0)),
        ],
        out_specs=[
            pl.BlockSpec((tm, hn), lambda i: (i, 0)),
            pl.BlockSpec((tm, hr), lambda i: (i, 0)),
        ],
        out_shape=[jax.ShapeDtypeStruct((t, hn), BF16), jax.ShapeDtypeStruct((t, hr), BF16)],
        compiler_params=_cparams(("parallel",)),
        name="qproj",
    )(z, g, wn, wr, wrr, cos2, sin2)


def _kvproj_kernel(ckv_ref, g_ref, wk_ref, wvt_ref, kr_ref, krr_ref, cos_ref, sin_ref,
                   kn_ref, vt_ref, kro_ref):
    ckvn = _rms(ckv_ref[...].astype(F32), g_ref[...]).astype(BF16)
    kn_ref[...] = jnp.dot(ckvn, wk_ref[...], preferred_element_type=F32).astype(BF16)
    vt = lax.dot_general(wvt_ref[...], ckvn, (((1,), (1,)), ((), ())), preferred_element_type=F32)
    vt_ref[:, :MLA_V, :] = vt.reshape(MLA_HEADS, MLA_V, vt.shape[1]).astype(BF16)
    vt_ref[:, MLA_V:, :] = jnp.ones((MLA_HEADS, MLA_VX - MLA_V, vt.shape[1]), BF16)
    kro_ref[...] = (kr_ref[...].astype(F32) * cos_ref[...]
                    + krr_ref[...].astype(F32) * sin_ref[...]).astype(BF16)


def _kvproj(z, g, wk, wvt, cos2, sin2, batch, seq):
    t = z.shape[0]
    tm = MLA_TILE
    spb = seq // tm
    hn = MLA_HEADS * MLA_NOPE
    hv = MLA_HEADS * MLA_V
    return pl.pallas_call(
        _kvproj_kernel,
        grid=(t // tm,),
        in_specs=[
            pl.BlockSpec((pl.Element(tm), pl.Element(MLA_KV_RANK)), lambda i: (i * tm, Z_CKV)),
            pl.BlockSpec((1, MLA_KV_RANK), lambda i: (0, 0)),
            pl.BlockSpec((MLA_KV_RANK, hn), lambda i: (0, 0)),
            pl.BlockSpec((hv, MLA_KV_RANK), lambda i: (0, 0)),
            pl.BlockSpec((tm, LANES), lambda i: (i, Z_KR // LANES)),
            pl.BlockSpec((tm, LANES), lambda i: (i, Z_KRR // LANES)),
            pl.BlockSpec((tm, LANES), lambda i: (i % spb, 0)),
            pl.BlockSpec((tm, LANES), lambda i: (i % spb, 0)),
        ],
        out_specs=[
            pl.BlockSpec((tm, hn), lambda i: (i, 0)),
            pl.BlockSpec((None, MLA_HEADS, None, MLA_VX, tm), lambda i: (i // spb, 0, i % spb, 0, 0)),
            pl.BlockSpec((tm, LANES), lambda i: (i, 0)),
        ],
        out_shape=[jax.ShapeDtypeStruct((t, hn), BF16),
                   jax.ShapeDtypeStruct((batch, MLA_HEADS, spb, MLA_VX, tm), BF16),
                   jax.ShapeDtypeStruct((t, LANES), BF16)],
        compiler_params=_cparams(("parallel",)),
        name="kvproj",
    )(z, g, wk, wvt, z, z, cos2, sin2)


def _mla_kernel(qn_ref, qr_ref, kn_ref, kr_ref, vt_ref, o_ref, q_ref, s_ref, p_ref, acc_ref,
                *, tile, nq):
    head = pl.program_id(1)
    lo = (head % 2) * MLA_ROPE
    half = tile // 2

    def scores(q, k):
        k0 = k * tile if isinstance(k, int) else pl.multiple_of(k * tile, tile)
        kc = jnp.concatenate([kn_ref[pl.ds(k0, tile), :], kr_ref[pl.ds(k0, tile), :]], axis=1)
        return lax.dot_general(kc, q, (((1,), (1,)), ((), ())), preferred_element_type=F32)

    def softmax(s, m):
        m_new = jnp.maximum(m, jnp.max(s, axis=0, keepdims=True))
        return jnp.exp2(s - m_new).astype(BF16), jnp.exp2(m - m_new), m_new

    def values(p, k):
        return jnp.dot(vt_ref[k], p, preferred_element_type=F32)

    for qi in range(nq):
        rows = slice(qi * tile, (qi + 1) * tile)
        qr = qr_ref[rows, :]
        lane = lax.broadcasted_iota(jnp.int32, qr.shape, 1)
        keep = (lane >= lo) & (lane < lo + MLA_ROPE)
        q_ref[...] = jnp.concatenate(
            [qn_ref[rows, :], jnp.where(keep, qr, jnp.zeros_like(qr))], axis=1)

        def put_scores(k):
            slot = k % 3
            if k < qi:
                s_ref[slot] = scores(q_ref[...], k)
                return
            kc = jnp.concatenate([kn_ref[rows, :], kr_ref[rows, :]], axis=1)
            nt = (((1,), (1,)), ((), ()))
            s_ref[slot, :half, :] = lax.dot_general(kc[:half], q_ref[...], nt,
                                                    preferred_element_type=F32)
            s_ref[slot, half:, half:] = lax.dot_general(kc[half:], q_ref[half:, :], nt,
                                                        preferred_element_type=F32)

        put_scores(0)
        m = jnp.full((1, tile), NEG, F32)
        for k in range(qi):
            put_scores(k + 1)
            p, a, m = softmax(s_ref[k % 3], m)
            p_ref[k % 2] = p
            pv = values(p_ref[k % 2], k)
            acc_ref[...] = pv if k == 0 else a * acc_ref[...] + pv

        slot = qi % 3
        key = lax.broadcasted_iota(jnp.int32, (half, half), 0)
        qry = lax.broadcasted_iota(jnp.int32, (half, half), 1)
        s_tl = jnp.where(key <= qry, s_ref[slot, :half, :half], NEG)
        s_tr = s_ref[slot, :half, half:]
        s_br = jnp.where(key <= qry, s_ref[slot, half:, half:], NEG)
        m_l = jnp.maximum(m[:, :half], jnp.max(s_tl, axis=0, keepdims=True))
        m_r = jnp.maximum(jnp.maximum(m[:, half:], jnp.max(s_tr, axis=0, keepdims=True)),
                          jnp.max(s_br, axis=0, keepdims=True))
        a = jnp.exp2(m - jnp.concatenate([m_l, m_r], axis=1))
        p_top = jnp.concatenate([jnp.exp2(s_tl - m_l), jnp.exp2(s_tr - m_r)], axis=1).astype(BF16)
        p_bot = jnp.exp2(s_br - m_r).astype(BF16)
        vt = vt_ref[qi]
        pv = jnp.dot(vt[:, :half], p_top, preferred_element_type=F32)
        pv_r = jnp.dot(vt[:, half:], p_bot, preferred_element_type=F32)
        acc = pv if qi == 0 else a * acc_ref[...] + pv
        num = jnp.concatenate([acc[:MLA_V, :half], acc[:MLA_V, half:] + pv_r[:MLA_V, :]], axis=1)
        den = jnp.concatenate([acc[MLA_V:MLA_V + 1, :half],
                               acc[MLA_V:MLA_V + 1, half:] + pv_r[MLA_V:MLA_V + 1, :]], axis=1)
        o_ref[rows, :] = (num / den).T.astype(o_ref.dtype)


def _mla(qn, qr, kn, kr2, vt, batch, seq):
    tile = MLA_TILE
    nq = seq // tile
    t = batch * seq
    tok = lambda b, h: (b, h)
    return pl.pallas_call(
        functools.partial(_mla_kernel, tile=tile, nq=nq),
        grid=(batch, MLA_HEADS),
        in_specs=[
            pl.BlockSpec((seq, LANES), tok),
            pl.BlockSpec((seq, LANES), lambda b, h: (b, h // 2)),
            pl.BlockSpec((seq, LANES), tok),
            pl.BlockSpec((seq, LANES), lambda b, h: (b, 0)),
            pl.BlockSpec((None, None, nq, MLA_VX, tile), lambda b, h: (b, h, 0, 0, 0)),
        ],
        out_specs=pl.BlockSpec((seq, LANES), tok),
        out_shape=jax.ShapeDtypeStruct((t, MLA_HEADS * MLA_V), BF16),
        scratch_shapes=[
            pltpu.VMEM((tile, 2 * LANES), BF16),
            pltpu.VMEM((3, tile, tile), F32),
            pltpu.VMEM((2, tile, tile), BF16),
            pltpu.VMEM((MLA_VX, tile), F32),
        ],
        compiler_params=_cparams(("parallel", "arbitrary")),
        name="mla",
    )(qn, qr, kn, kr2, vt)


def _t5_bucket(dist):
    max_exact = REL_BUCKETS // 2
    n = jnp.maximum(dist, 0)
    large = max_exact + (jnp.log(jnp.maximum(n, 1).astype(F32) / max_exact)
                         / math.log(REL_MAX_DIST / max_exact)
                         * (REL_BUCKETS - max_exact)).astype(jnp.int32)
    large = jnp.minimum(large, REL_BUCKETS - 1)
    return jnp.where(n < max_exact, n, large)


def _t5bias_kernel(rb_ref, bucket_ref, o_ref):
    kvh = pl.program_id(0)
    bucket = bucket_ref[...]
    key = lax.broadcasted_iota(jnp.int32, bucket.shape, 0)
    qry = lax.broadcasted_iota(jnp.int32, bucket.shape, 1)
    dist = BLOCK + qry - key
    mask = (dist >= 0) & (dist < WINDOW)
    mask_first = mask & (key >= BLOCK)
    for g in range(SWA_GROUP):
        acc = jnp.zeros(bucket.shape, F32)
        for b in range(REL_BUCKETS):
            acc = jnp.where(bucket == b, rb_ref[b, kvh * SWA_GROUP + g], acc)
        cols = slice(g * BLOCK, (g + 1) * BLOCK)
        o_ref[0, :, cols] = jnp.where(mask, acc * LOG2E, NEG)
        o_ref[1, :, cols] = jnp.where(mask_first, acc * LOG2E, NEG)


def _t5bias(rel_bias):
    a = jnp.arange(BLOCK)
    bidx = jnp.arange(2 * BLOCK)
    bucket = _t5_bucket(BLOCK + a[None, :] - bidx[:, None]).astype(jnp.int32)
    return pl.pallas_call(
        _t5bias_kernel,
        grid=(SWA_KV_HEADS,),
        in_specs=[
            pl.BlockSpec(memory_space=pltpu.SMEM),
            pl.BlockSpec((2 * BLOCK, BLOCK), lambda h: (0, 0)),
        ],
        out_specs=pl.BlockSpec((2, None, 2 * BLOCK, SWA_GROUP * BLOCK), lambda h: (0, h, 0, 0)),
        out_shape=jax.ShapeDtypeStruct((2, SWA_KV_HEADS, 2 * BLOCK, SWA_GROUP * BLOCK), F32),
        compiler_params=_cparams(("arbitrary",)),
        name="t5bias",
    )(rel_bias, bucket)


def _roll_half(x):
    return pltpu.bitcast(pltpu.roll(pltpu.bitcast(x, jnp.uint32), LANES // 2, 1), x.dtype)


def _keep_lanes(x, word_mask):
    return pltpu.bitcast(pltpu.bitcast(x, jnp.uint32) & word_mask, x.dtype)


def _swa_kernel(sink_ref, q_ref, kp_ref, kc_ref, vp_ref, vc_ref, bias_ref, o_ref, *, nqb):
    seq_start = jnp.where(pl.program_id(1) == 0, 1, 0)
    k_all = jnp.concatenate([kp_ref[...], kc_ref[...]], axis=0)
    v_all = jnp.concatenate([vp_ref[...], vc_ref[...]], axis=0)
    vt_all = v_all.astype(F32).T.astype(BF16)
    low = lax.broadcasted_iota(jnp.int32, (BLOCK, LANES), 1) < SWA_HEAD_DIM
    low2 = lax.broadcasted_iota(jnp.int32, (2 * BLOCK, LANES), 1) < SWA_HEAD_DIM
    ones = jnp.ones((BF16_ROWS, 2 * BLOCK), BF16)
    lane = lax.broadcasted_iota(jnp.int32, (BLOCK // 2, LANES), 1)
    keep_low = jnp.where(lane < SWA_HEAD_DIM, jnp.uint32(0xFFFFFFFF), jnp.uint32(0))
    keep_high = ~keep_low
    for qb in range(nqb):
        first = seq_start if qb == 0 else 0
        q = q_ref[qb * BLOCK:(qb + 1) * BLOCK, :]
        k = k_all[qb * BLOCK:(qb + 2) * BLOCK, :]
        vt = vt_all[:, qb * BLOCK:(qb + 2) * BLOCK]
        outs = []
        for kvh in range(SWA_KV_HEADS):
            pair = slice(LANES * (kvh // 2), LANES * (kvh // 2 + 1))
            kb = k[:, pair]
            kr = _roll_half(kb)
            kd = jnp.where(low2, kr, kb) if kvh % 2 else jnp.where(low2, kb, kr)
            vth = vt[SWA_HEAD_DIM * kvh:SWA_HEAD_DIM * (kvh + 1), :]
            vtx = jnp.concatenate([vth, vth, ones], axis=0)
            qs, sinks = [], []
            for g in range(SWA_GROUP):
                hq = kvh * SWA_GROUP + g
                blk = q[:, LANES * (hq // 2):LANES * (hq // 2 + 1)]
                qs.append(_keep_lanes(blk, keep_low if hq % 2 == 0 else keep_high))
                sinks.append(jnp.full((1, BLOCK), sink_ref[hq] * LOG2E, F32))
            qg = jnp.concatenate(qs, axis=0)
            sink = jnp.concatenate(sinks, axis=1)
            s = lax.dot_general(kd, qg, (((1,), (1,)), ((), ())), preferred_element_type=F32)
            s = s + bias_ref[first, kvh]
            m = jnp.maximum(jnp.max(s, axis=0, keepdims=True), sink)
            e = jnp.exp2(s - m).astype(BF16)
            ox = jnp.dot(vtx, e, preferred_element_type=F32)
            denom = ox[2 * SWA_HEAD_DIM:2 * SWA_HEAD_DIM + 1, :] + jnp.exp2(sink - m)
            o = ox[:2 * SWA_HEAD_DIM, :] / denom
            for j in range(SWA_GROUP // 2):
                a = o[:, 2 * j * BLOCK:(2 * j + 1) * BLOCK].T
                b = o[:, (2 * j + 1) * BLOCK:(2 * j + 2) * BLOCK].T
                outs.append(jnp.where(low, a, b).astype(BF16))
        o_ref[qb * BLOCK:(qb + 1) * BLOCK, :] = jnp.concatenate(outs, axis=1)


def _swa(z, bias, sinks, batch, seq):
    nqb = 4
    rows = nqb * BLOCK
    ns = seq // rows
    t = batch * seq
    qcols = SWA_HEADS * SWA_HEAD_DIM

    def cur(col):
        return pl.BlockSpec((rows, SWA_KV), lambda b, n: (b * ns + n, col))

    def prev(col):
        return pl.BlockSpec((BLOCK, SWA_KV),
                            lambda b, n: ((b * ns + n) * nqb - jnp.minimum(n, 1), col))

    return pl.pallas_call(
        functools.partial(_swa_kernel, nqb=nqb),
        grid=(batch, ns),
        in_specs=[
            pl.BlockSpec(memory_space=pltpu.SMEM),
            pl.BlockSpec((pl.Element(rows), pl.Element(qcols)),
                         lambda b, n: ((b * ns + n) * rows, Z_QS)),
            prev(Z_KS // SWA_KV), cur(Z_KS // SWA_KV),
            prev(Z_VS // SWA_KV), cur(Z_VS // SWA_KV),
            pl.BlockSpec(bias.shape, lambda b, n: (0, 0, 0, 0), pipeline_mode=pl.Buffered(1)),
        ],
        out_specs=pl.BlockSpec((rows, qcols), lambda b, n: (b * ns + n, 0)),
        out_shape=jax.ShapeDtypeStruct((t, qcols), BF16),
        compiler_params=_cparams(("parallel", "arbitrary")),
        name="swa",
    )(sinks, z, z, z, z, z, bias)


def _mix_kernel(oa_ref, ob_ref, ga_ref, gb_ref, w_ref, x_ref, g_ref, g2_ref, mod_ref, o_ref, h2_ref,
                *, tm, sub):
    for r in range(tm // sub):
        rows = slice(r * sub, (r + 1) * sub)
        ga = jax.nn.sigmoid(ga_ref[rows, :].astype(F32))
        gb = jax.nn.sigmoid(gb_ref[rows, :].astype(F32))
        u = ga * oa_ref[rows, :].astype(F32) + gb * ob_ref[rows, :].astype(F32)
        mix = jnp.dot(u.astype(BF16), w_ref[...], preferred_element_type=F32)
        x1 = x_ref[rows, :] + mod_ref[GT1:GT1 + 1, :] * _rms(mix, g_ref[...])
        o_ref[rows, :] = x1
        h2 = _rms(x1, g2_ref[...]) * (1.0 + mod_ref[SC2:SC2 + 1, :]) + mod_ref[SH2:SH2 + 1, :]
        h2_ref[rows, :] = h2.astype(BF16)


def _mix(oa, ob, z, w, x2, g, g2, mod, seq):
    t, d = x2.shape
    tm, sub = 512, 256
    tpb = seq // tm
    row = lambda i: (i, 0)
    return pl.pallas_call(
        functools.partial(_mix_kernel, tm=tm, sub=sub),
        grid=(t // tm,),
        in_specs=[
            pl.BlockSpec((tm, d), row),
            pl.BlockSpec((tm, d), row),
            pl.BlockSpec((tm, d), lambda i: (i, Z_GA // d)),
            pl.BlockSpec((tm, d), lambda i: (i, Z_GB // d)),
            pl.BlockSpec((d, d), lambda i: (0, 0), pipeline_mode=pl.Buffered(1)),
            pl.BlockSpec((tm, d), row),
            pl.BlockSpec((1, d), lambda i: (0, 0)),
            pl.BlockSpec((1, d), lambda i: (0, 0)),
            pl.BlockSpec((None, 6, d), lambda i: (i // tpb, 0, 0)),
        ],
        out_specs=[pl.BlockSpec((tm, d), row), pl.BlockSpec((tm, d), row)],
        out_shape=[jax.ShapeDtypeStruct((t, d), F32), jax.ShapeDtypeStruct((t, d), BF16)],
        compiler_params=_cparams(("parallel",)),
        name="mix",
    )(oa, ob, z, z, w, x2, g, g2, mod)


def _ffn_up_kernel(h2_ref, halo_ref, wg32_ref, wv32_ref, cwg_ref, cwv_ref, cbg_ref, cbv_ref,
                   o_ref, h_ref, wg_ref, wv_ref, *, tm, tn, sub, tpb):
    i = pl.program_id(1)

    @pl.when(i == 0)
    def _():
        wg_ref[...] = wg32_ref[...].astype(BF16)
        wv_ref[...] = wv32_ref[...].astype(BF16)

    halo = halo_ref[...]
    h_ref[0:BF16_ROWS, :] = jnp.where(i % tpb == 0, jnp.zeros_like(halo), halo)
    h_ref[BF16_ROWS:, :] = h2_ref[...]
    h = h_ref[...]

    rows = tm + BF16_ROWS
    blocks = 2 * rows // BF16_ROWS

    def pair(a, b):
        a8 = jnp.broadcast_to(a, (8, sub))
        b8 = jnp.broadcast_to(b, (8, sub))
        return pltpu.bitcast(pltpu.pack_elementwise([a8, b8], packed_dtype=BF16), BF16)

    def shift(xb):
        xi = pltpu.bitcast(xb.reshape(2 * rows, sub), jnp.int32)
        return pltpu.bitcast(pltpu.roll(xi, 1, 0), BF16).reshape(blocks, BF16_ROWS, sub)

    for c in range(tn // sub):
        cols = slice(c * sub, (c + 1) * sub)
        ug = jnp.dot(h, wg_ref[:, cols], preferred_element_type=F32)
        uv = jnp.dot(h, wv_ref[:, cols], preferred_element_type=F32)
        p = pltpu.bitcast(pltpu.pack_elementwise([ug, uv], packed_dtype=BF16), BF16)
        p = p.reshape(blocks, BF16_ROWS, sub)
        y = p * pair(cwg_ref[1:2, cols], cwv_ref[1:2, cols]) + shift(
            p * pair(cwg_ref[0:1, cols], cwv_ref[0:1, cols]))
        y = p * pair(cwg_ref[2:3, cols], cwv_ref[2:3, cols]) + shift(y)
        y = y + pair(cbg_ref[:, cols], cbv_ref[:, cols])
        yi = pltpu.bitcast(y.reshape(2 * rows, sub), jnp.int32)[BF16_ROWS:, :]
        gate = pltpu.unpack_elementwise(yi, index=0, packed_dtype=BF16, unpacked_dtype=F32)
        val = pltpu.unpack_elementwise(yi, index=1, packed_dtype=BF16, unpacked_dtype=F32)
        o_ref[:, cols] = (gate * jax.nn.sigmoid(gate) * val).astype(o_ref.dtype)


def _ffn_up(h2, w_up, conv_w, conv_b, seq):
    t, d = h2.shape
    tm, tn, sub = 1024, 512, 256
    tpb = seq // tm
    nj = D_FF // tn
    hb = tm // BF16_ROWS
    return pl.pallas_call(
        functools.partial(_ffn_up_kernel, tm=tm, tn=tn, sub=sub, tpb=tpb),
        grid=(nj, t // tm),
        in_specs=[
            pl.BlockSpec((tm, d), lambda j, i: (i, 0)),
            pl.BlockSpec((BF16_ROWS, d), lambda j, i: (jnp.maximum(i * hb - 1, 0), 0)),
            pl.BlockSpec((d, tn), lambda j, i: (0, j)),
            pl.BlockSpec((d, tn), lambda j, i: (0, j + nj)),
            pl.BlockSpec((CONV_WIDTH, tn), lambda j, i: (0, j)),
            pl.BlockSpec((CONV_WIDTH, tn), lambda j, i: (0, j + nj)),
            pl.BlockSpec((1, tn), lambda j, i: (0, j)),
            pl.BlockSpec((1, tn), lambda j, i: (0, j + nj)),
        ],
        out_specs=pl.BlockSpec((tm, tn), lambda j, i: (i, j)),
        out_shape=jax.ShapeDtypeStruct((t, D_FF), BF16),
        scratch_shapes=[pltpu.VMEM((tm + BF16_ROWS, d), BF16),
                        pltpu.VMEM((d, tn), BF16), pltpu.VMEM((d, tn), BF16)],
        compiler_params=_cparams(("arbitrary", "arbitrary")),
        name="ffn_up",
    )(h2, h2, w_up, w_up, conv_w, conv_w, conv_b, conv_b)


def _ffn_down_kernel(a_ref, w_ref, x_ref, g_ref, mod_ref, o_ref):
    y = jnp.dot(a_ref[...], w_ref[...], preferred_element_type=F32)
    o_ref[...] = x_ref[...] + mod_ref[GT2:GT2 + 1, :] * _rms(y, g_ref[...])


def _ffn_down(a, w, x1, g, mod, seq):
    t, d = x1.shape
    tm = 256
    tpb = seq // tm
    return pl.pallas_call(
        _ffn_down_kernel,
        grid=(t // tm,),
        in_specs=[
            pl.BlockSpec((tm, D_FF), lambda i: (i, 0)),
            pl.BlockSpec((D_FF, d), lambda i: (0, 0), pipeline_mode=pl.Buffered(1)),
            pl.BlockSpec((tm, d), lambda i: (i, 0)),
            pl.BlockSpec((1, d), lambda i: (0, 0)),
            pl.BlockSpec((None, 6, d), lambda i: (i // tpb, 0, 0)),
        ],
        out_specs=pl.BlockSpec((tm, d), lambda i: (i, 0)),
        out_shape=jax.ShapeDtypeStruct((t, d), F32),
        compiler_params=_cparams(("parallel",)),
        name="ffn_down",
    )(a, w, x1, g, mod)


def _rot_cols(w):
    half = MLA_ROPE // 2
    w = w.reshape(w.shape[0], -1, MLA_ROPE)
    return jnp.concatenate([-w[..., half:], w[..., :half]], axis=-1).reshape(w.shape[0], -1)


def _prep_w_uq(w_uq):
    w = (w_uq * (MLA_QK ** -0.5 * LOG2E)).reshape(MLA_Q_RANK, MLA_HEADS, MLA_QK)
    wn = w[..., :MLA_NOPE].reshape(MLA_Q_RANK, -1)
    wr = w[..., MLA_NOPE:].reshape(MLA_Q_RANK, -1)
    return wn.astype(BF16), wr.astype(BF16), _rot_cols(wr).astype(BF16)


def _prep_w_ukv(w_ukv):
    w = w_ukv.reshape(MLA_KV_RANK, MLA_HEADS, MLA_NOPE + MLA_V)
    wk = w[..., :MLA_NOPE].reshape(MLA_KV_RANK, -1)
    wvt = w[..., MLA_NOPE:].reshape(MLA_KV_RANK, -1).T
    return wk.astype(BF16), wvt.astype(BF16)


def _rope_tables2(seq):
    pos = jnp.arange(seq, dtype=F32)
    inv = ROPE_THETA ** (-jnp.arange(0, MLA_ROPE, 2, dtype=F32) / MLA_ROPE)
    ang = pos[:, None] * inv[None, :]
    ang = jnp.concatenate([ang, ang, ang, ang], axis=-1)
    return jnp.cos(ang), jnp.sin(ang)


def kernel(x, c, w_ada, b_ada, g_pre_mix, g_post_mix, w_in, g_q_lat, w_uq, g_kv_lat, w_ukv,
           rel_bias, sinks, w_o, g_pre_ffn, g_post_ffn, w_up, conv_w, conv_b, w_down):
    batch, seq, d = x.shape
    depth = w_ada.shape[0]
    xt = x.reshape(batch * seq, d)
    c8 = jnp.pad(c, ((0, 8 - batch), (0, 0)))
    cos2, sin2 = _rope_tables2(seq)
    bias = _t5bias(rel_bias)
    for l in range(depth):
        mod = _ada(c8, w_ada[l], b_ada[l][None, :])[:batch].reshape(batch, 6, d)
        z = _inproj(_prenorm(xt, g_pre_mix[l][None, :], mod, seq), w_in[l].T)
        wn, wr, wrr = _prep_w_uq(w_uq[l])
        qn, qr = _qproj(z, g_q_lat[l][None, :], wn, wr, wrr, cos2, sin2, seq)
        wk, wvt = _prep_w_ukv(w_ukv[l])
        kn, vt, kr2 = _kvproj(z, g_kv_lat[l][None, :], wk, wvt, cos2, sin2, batch, seq)
        o_a = _mla(qn, qr, kn, kr2, vt, batch, seq)
        o_b = _swa(z, bias, sinks[l], batch, seq)
        x1, h2 = _mix(o_a, o_b, z, w_o[l].astype(BF16), xt, g_post_mix[l][None, :],
                      g_pre_ffn[l][None, :], mod, seq)
        a = _ffn_up(h2, w_up[l], conv_w[l], conv_b[l][None, :], seq)
        xt = _ffn_down(a, w_down[l].astype(BF16), x1, g_post_ffn[l][None, :], mod, seq)
    return xt.reshape(batch, seq, d)
```

```python
import functools
import math

import jax
import jax.numpy as jnp
from jax import lax
from jax.experimental import pallas as pl
from jax.experimental.pallas import tpu as pltpu

F32 = jnp.float32
BF16 = jnp.bfloat16

D_MODEL = 2048
MLA_NOPE = 128
MLA_ROPE = 64
MLA_V = 128
MLA_HEADS = D_MODEL // MLA_V
MLA_Q_RANK = 768
MLA_KV_RANK = 512
MLA_QK = MLA_NOPE + MLA_ROPE
ROPE_THETA = 10000.0
SWA_HEAD_DIM = 64
SWA_HEADS = D_MODEL // SWA_HEAD_DIM
SWA_KV_HEADS = 4
SWA_GROUP = SWA_HEADS // SWA_KV_HEADS
SWA_KV = SWA_KV_HEADS * SWA_HEAD_DIM
WINDOW = 128
BLOCK = 128
REL_BUCKETS = 32
REL_MAX_DIST = 128
D_FF = 5632
CONV_WIDTH = 3
EPS = 1e-6
NEG = -1e30
LOG2E = 1.4426950408889634

VMEM_LIMIT_BYTES = 56 * 1024 * 1024
LANES = 128
BF16_ROWS = 16
MLA_TILE = 512
MLA_VX = MLA_V + BF16_ROWS

Z_CQ = 0
Z_CKV = 768
Z_KR = 1280
Z_KRR = 1408
Z_QS = 1536
Z_KS = 3584
Z_VS = 3840
Z_GA = 4096
Z_GB = 6144
Z_COLS = 8192
W_IN_KR = MLA_Q_RANK + MLA_KV_RANK
W_IN_QS = W_IN_KR + MLA_ROPE

SH1, SC1, GT1, SH2, SC2, GT2 = range(6)


def _cparams(sem):
    return pltpu.CompilerParams(dimension_semantics=sem, vmem_limit_bytes=VMEM_LIMIT_BYTES)


def _rms(x, g):
    ms = jnp.mean(x * x, axis=-1, keepdims=True)
    return x * lax.rsqrt(ms + EPS) * g


def _ada_kernel(c_ref, w_ref, b_ref, o_ref):
    c = c_ref[...]
    ca = (c * jax.nn.sigmoid(c)).astype(BF16)
    o_ref[...] = jnp.dot(ca, w_ref[...].astype(BF16), preferred_element_type=F32) + b_ref[...]


def _ada(c8, w_ada, b_ada):
    n = w_ada.shape[1]
    tn = 1024
    return pl.pallas_call(
        _ada_kernel,
        grid=(n // tn,),
        in_specs=[
            pl.BlockSpec((8, D_MODEL), lambda j: (0, 0)),
            pl.BlockSpec((D_MODEL, tn), lambda j: (0, j)),
            pl.BlockSpec((1, tn), lambda j: (0, j)),
        ],
        out_specs=pl.BlockSpec((8, tn), lambda j: (0, j)),
        out_shape=jax.ShapeDtypeStruct((8, n), F32),
        compiler_params=_cparams(("arbitrary",)),
        name="ada",
    )(c8, w_ada, b_ada)


def _fill_normed(x_ref, g, sc, sh, h_ref, row0, nrows, chunk):
    def body(i, carry):
        r = pl.multiple_of(i * chunk, chunk)
        y = _rms(x_ref[pl.ds(r, chunk), :], g)
        h_ref[pl.ds(row0 + r, chunk), :] = (y * (1.0 + sc) + sh).astype(BF16)
        return carry

    lax.fori_loop(0, nrows // chunk, body, 0)


def _prenorm_kernel(x_ref, g_ref, mod_ref, o_ref, *, tm):
    _fill_normed(x_ref, g_ref[...], mod_ref[SC1:SC1 + 1, :], mod_ref[SH1:SH1 + 1, :],
                 o_ref, 0, tm, 128)


def _prenorm(x2, g, mod, seq):
    t, d = x2.shape
    tm = 1024
    tpb = seq // tm
    return pl.pallas_call(
        functools.partial(_prenorm_kernel, tm=tm),
        grid=(t // tm,),
        in_specs=[
            pl.BlockSpec((tm, d), lambda i: (i, 0)),
            pl.BlockSpec((1, d), lambda i: (0, 0)),
            pl.BlockSpec((None, 6, d), lambda i: (i // tpb, 0, 0)),
        ],
        out_specs=pl.BlockSpec((tm, d), lambda i: (i, 0)),
        out_shape=jax.ShapeDtypeStruct((t, d), BF16),
        compiler_params=_cparams(("parallel",)),
        name="prenorm",
    )(x2, g, mod)


INPROJ_TN = 512
KR_TILE = Z_KR // INPROJ_TN
QS_TILES = (Z_QS // INPROJ_TN, Z_KS // INPROJ_TN)


def _inproj_kernel(h_ref, *refs, nw):
    wt_refs, o_ref, wb_ref = refs[:nw], refs[nw], refs[nw + 1]
    j = pl.program_id(0)

    @pl.when(pl.program_id(1) == 0)
    def _():
        for w, wt_ref in enumerate(wt_refs):
            win = j * nw + w
            rows = slice(w * INPROJ_TN, (w + 1) * INPROJ_TN)
            is_qs = (win >= QS_TILES[0]) & (win < QS_TILES[1])
            scale = jnp.where(is_qs, SWA_HEAD_DIM ** -0.5 * LOG2E, 1.0).astype(F32)
            wb_ref[rows, :] = (wt_ref[...] * scale).astype(BF16)
            if w != KR_TILE % nw:
                continue

            @pl.when(win == KR_TILE)
            def _():
                lo = Z_KR - KR_TILE * INPROJ_TN
                kr = wt_ref[lo:lo + MLA_ROPE, :]
                half = MLA_ROPE // 2
                rot = jnp.concatenate([-kr[half:, :], kr[:half, :]], axis=0)
                wb_ref[w * INPROJ_TN + lo:(w + 1) * INPROJ_TN, :] = jnp.concatenate(
                    [kr, kr, rot, rot], axis=0).astype(BF16)

    o_ref[...] = lax.dot_general(h_ref[...], wb_ref[...], (((1,), (1,)), ((), ())),
                                 preferred_element_type=F32).astype(o_ref.dtype)


def _inproj(h, wt):
    t, d = h.shape
    tm, nw = 1024, 2
    tn = nw * INPROJ_TN

    def src_row(win):
        u = MLA_ROPE
        k = INPROJ_TN // u
        return jnp.where(win <= KR_TILE, win * k, W_IN_QS // u + (win - KR_TILE - 1) * k) * u

    def wspec(w):
        return pl.BlockSpec((pl.Element(INPROJ_TN), pl.Element(d)),
                            lambda j, i: (src_row(j * nw + w), 0))

    return pl.pallas_call(
        functools.partial(_inproj_kernel, nw=nw),
        grid=(Z_COLS // tn, t // tm),
        in_specs=[pl.BlockSpec((tm, d), lambda j, i: (i, 0))] + [wspec(w) for w in range(nw)],
        out_specs=pl.BlockSpec((tm, tn), lambda j, i: (i, j)),
        out_shape=jax.ShapeDtypeStruct((t, Z_COLS), BF16),
        scratch_shapes=[pltpu.VMEM((tn, d), BF16)],
        compiler_params=_cparams(("arbitrary", "arbitrary")),
        name="inproj",
    )(h, *([wt] * nw))


def _qproj_kernel(cq_ref, g_ref, wn_ref, wr_ref, cos_ref, sin_ref, qn_ref, qr_ref):
    cqn = _rms(cq_ref[...].astype(F32), g_ref[...]).astype(BF16)
    qn_ref[...] = jnp.dot(cqn, wn_ref[...], preferred_element_type=F32).astype(BF16)
    a = jnp.dot(cqn, wr_ref[...], preferred_element_type=F32)
    cos = cos_ref[...]
    lane = lax.broadcasted_iota(jnp.int32, cos.shape, 1)
    first = (lane % MLA_ROPE) < MLA_ROPE // 2
    sin = jnp.where(first, -sin_ref[...], sin_ref[...])
    outs = []
    for i in range(a.shape[1] // LANES):
        x = a[:, LANES * i:LANES * (i + 1)]
        rot = jnp.where(first, pltpu.roll(x, LANES - MLA_ROPE // 2, 1),
                        pltpu.roll(x, MLA_ROPE // 2, 1))
        outs.append(x * cos + rot * sin)
    qr_ref[...] = jnp.concatenate(outs, axis=1).astype(BF16)


def _qproj(z, g, wn, wr, cos2, sin2, seq):
    t = z.shape[0]
    tm = 1024
    spb = seq // tm
    hn = MLA_HEADS * MLA_NOPE
    hr = MLA_HEADS * MLA_ROPE
    return pl.pallas_call(
        _qproj_kernel,
        grid=(t // tm,),
        in_specs=[
            pl.BlockSpec((tm, MLA_Q_RANK), lambda i: (i, Z_CQ // MLA_Q_RANK)),
            pl.BlockSpec((1, MLA_Q_RANK), lambda i: (0, 0)),
            pl.BlockSpec((MLA_Q_RANK, hn), lambda i: (0, 0)),
            pl.BlockSpec((MLA_Q_RANK, hr), lambda i: (0, 0)),
            pl.BlockSpec((tm, LANES), lambda i: (i % spb, 0)),
            pl.BlockSpec((tm, LANES), lambda i: (i % spb, 0)),
        ],
        out_specs=[
            pl.BlockSpec((tm, hn), lambda i: (i, 0)),
            pl.BlockSpec((tm, hr), lambda i: (i, 0)),
        ],
        out_shape=[jax.ShapeDtypeStruct((t, hn), BF16), jax.ShapeDtypeStruct((t, hr), BF16)],
        compiler_params=_cparams(("parallel",)),
        name="qproj",
    )(z, g, wn, wr, cos2, sin2)


def _kvproj_kernel(ckv_ref, g_ref, wk_ref, wvt_ref, kr_ref, krr_ref, cos_ref, sin_ref,
                   kn_ref, vt_ref, kro_ref):
    ckvn = _rms(ckv_ref[...].astype(F32), g_ref[...]).astype(BF16)
    kn_ref[...] = jnp.dot(ckvn, wk_ref[...], preferred_element_type=F32).astype(BF16)
    vt = lax.dot_general(wvt_ref[...], ckvn, (((1,), (1,)), ((), ())), preferred_element_type=F32)
    vt_ref[:, :MLA_V, :] = vt.reshape(MLA_HEADS, MLA_V, vt.shape[1]).astype(BF16)
    vt_ref[:, MLA_V:, :] = jnp.ones((MLA_HEADS, MLA_VX - MLA_V, vt.shape[1]), BF16)
    kro_ref[...] = (kr_ref[...].astype(F32) * cos_ref[...]
                    + krr_ref[...].astype(F32) * sin_ref[...]).astype(BF16)


def _kvproj(z, g, wk, wvt, cos2, sin2, batch, seq):
    t = z.shape[0]
    tm = MLA_TILE
    spb = seq // tm
    hn = MLA_HEADS * MLA_NOPE
    hv = MLA_HEADS * MLA_V
    return pl.pallas_call(
        _kvproj_kernel,
        grid=(t // tm,),
        in_specs=[
            pl.BlockSpec((pl.Element(tm), pl.Element(MLA_KV_RANK)), lambda i: (i * tm, Z_CKV)),
            pl.BlockSpec((1, MLA_KV_RANK), lambda i: (0, 0)),
            pl.BlockSpec((MLA_KV_RANK, hn), lambda i: (0, 0)),
            pl.BlockSpec((hv, MLA_KV_RANK), lambda i: (0, 0)),
            pl.BlockSpec((tm, LANES), lambda i: (i, Z_KR // LANES)),
            pl.BlockSpec((tm, LANES), lambda i: (i, Z_KRR // LANES)),
            pl.BlockSpec((tm, LANES), lambda i: (i % spb, 0)),
            pl.BlockSpec((tm, LANES), lambda i: (i % spb, 0)),
        ],
        out_specs=[
            pl.BlockSpec((tm, hn), lambda i: (i, 0)),
            pl.BlockSpec((None, MLA_HEADS, None, MLA_VX, tm), lambda i: (i // spb, 0, i % spb, 0, 0)),
            pl.BlockSpec((tm, LANES), lambda i: (i, 0)),
        ],
        out_shape=[jax.ShapeDtypeStruct((t, hn), BF16),
                   jax.ShapeDtypeStruct((batch, MLA_HEADS, spb, MLA_VX, tm), BF16),
                   jax.ShapeDtypeStruct((t, LANES), BF16)],
        compiler_params=_cparams(("parallel",)),
        name="kvproj",
    )(z, g, wk, wvt, z, z, cos2, sin2)


def _mla_kernel(qn_ref, qr_ref, kn_ref, kr_ref, vt_ref, o_ref, q_ref, s_ref, p_ref, acc_ref,
                *, tile, nq):
    head = pl.program_id(1)
    lo = (head % 2) * MLA_ROPE
    half = tile // 2

    def scores(q, k):
        k0 = k * tile if isinstance(k, int) else pl.multiple_of(k * tile, tile)
        kc = jnp.concatenate([kn_ref[pl.ds(k0, tile), :], kr_ref[pl.ds(k0, tile), :]], axis=1)
        return lax.dot_general(kc, q, (((1,), (1,)), ((), ())), preferred_element_type=F32)

    def softmax(s, m):
        m_new = jnp.maximum(m, jnp.max(s, axis=0, keepdims=True))
        return jnp.exp2(s - m_new).astype(BF16), jnp.exp2(m - m_new), m_new

    def values(p, k):
        return jnp.dot(vt_ref[k], p, preferred_element_type=F32)

    tiles = [(qi, k) for qi in range(nq) for k in range(qi + 1)]
    nt = (((1,), (1,)), ((), ()))

    def put_scores(g):
        qi, k = tiles[g]
        rows = slice(qi * tile, (qi + 1) * tile)
        if k == 0:
            qr = qr_ref[rows, :]
            lane = lax.broadcasted_iota(jnp.int32, qr.shape, 1)
            keep = (lane >= lo) & (lane < lo + MLA_ROPE)
            q_ref[qi % 2] = jnp.concatenate(
                [qn_ref[rows, :], jnp.where(keep, qr, jnp.zeros_like(qr))], axis=1)
        q = q_ref[qi % 2]
        if k < qi:
            s_ref[g % 3] = scores(q, k)
            return
        kc = jnp.concatenate([kn_ref[rows, :], kr_ref[rows, :]], axis=1)
        s_ref[g % 3, :half, :] = lax.dot_general(kc[:half], q, nt, preferred_element_type=F32)
        s_ref[g % 3, half:, half:] = lax.dot_general(kc[half:], q[half:, :], nt,
                                                     preferred_element_type=F32)

    put_scores(0)
    m = None
    for g, (qi, k) in enumerate(tiles):
        if g + 1 < len(tiles):
            put_scores(g + 1)
        if k == 0:
            m = jnp.full((1, tile), NEG, F32)
        slot = g % 3
        if k < qi:
            p, a, m = softmax(s_ref[slot], m)
            p_ref[k % 2] = p
            pv = values(p_ref[k % 2], k)
            acc_ref[...] = pv if k == 0 else a * acc_ref[...] + pv
            continue

        key = lax.broadcasted_iota(jnp.int32, (half, half), 0)
        qry = lax.broadcasted_iota(jnp.int32, (half, half), 1)
        s_tl = jnp.where(key <= qry, s_ref[slot, :half, :half], NEG)
        s_tr = s_ref[slot, :half, half:]
        s_br = jnp.where(key <= qry, s_ref[slot, half:, half:], NEG)
        m_l = jnp.maximum(m[:, :half], jnp.max(s_tl, axis=0, keepdims=True))
        m_r = jnp.maximum(jnp.maximum(m[:, half:], jnp.max(s_tr, axis=0, keepdims=True)),
                          jnp.max(s_br, axis=0, keepdims=True))
        a = jnp.exp2(m - jnp.concatenate([m_l, m_r], axis=1))
        p_top = jnp.concatenate([jnp.exp2(s_tl - m_l), jnp.exp2(s_tr - m_r)], axis=1).astype(BF16)
        p_bot = jnp.exp2(s_br - m_r).astype(BF16)
        vt = vt_ref[qi]
        pv = jnp.dot(vt[:, :half], p_top, preferred_element_type=F32)
        pv_r = jnp.dot(vt[:, half:], p_bot, preferred_element_type=F32)
        acc = pv if qi == 0 else a * acc_ref[...] + pv
        num = jnp.concatenate([acc[:MLA_V, :half], acc[:MLA_V, half:] + pv_r[:MLA_V, :]], axis=1)
        den = jnp.concatenate([acc[MLA_V:MLA_V + 1, :half],
                               acc[MLA_V:MLA_V + 1, half:] + pv_r[MLA_V:MLA_V + 1, :]], axis=1)
        o_ref[qi * tile:(qi + 1) * tile, :] = (num / den).T.astype(o_ref.dtype)


def _mla(qn, qr, kn, kr2, vt, batch, seq):
    tile = MLA_TILE
    nq = seq // tile
    t = batch * seq
    tok = lambda b, h: (b, h)
    return pl.pallas_call(
        functools.partial(_mla_kernel, tile=tile, nq=nq),
        grid=(batch, MLA_HEADS),
        in_specs=[
            pl.BlockSpec((seq, LANES), tok),
            pl.BlockSpec((seq, LANES), lambda b, h: (b, h // 2)),
            pl.BlockSpec((seq, LANES), tok),
            pl.BlockSpec((seq, LANES), lambda b, h: (b, 0)),
            pl.BlockSpec((None, None, nq, MLA_VX, tile), lambda b, h: (b, h, 0, 0, 0)),
        ],
        out_specs=pl.BlockSpec((seq, LANES), tok),
        out_shape=jax.ShapeDtypeStruct((t, MLA_HEADS * MLA_V), BF16),
        scratch_shapes=[
            pltpu.VMEM((2, tile, 2 * LANES), BF16),
            pltpu.VMEM((3, tile, tile), F32),
            pltpu.VMEM((2, tile, tile), BF16),
            pltpu.VMEM((MLA_VX, tile), F32),
        ],
        compiler_params=_cparams(("parallel", "arbitrary")),
        name="mla",
    )(qn, qr, kn, kr2, vt)


def _t5_bucket(dist):
    max_exact = REL_BUCKETS // 2
    n = jnp.maximum(dist, 0)
    large = max_exact + (jnp.log(jnp.maximum(n, 1).astype(F32) / max_exact)
                         / math.log(REL_MAX_DIST / max_exact)
                         * (REL_BUCKETS - max_exact)).astype(jnp.int32)
    large = jnp.minimum(large, REL_BUCKETS - 1)
    return jnp.where(n < max_exact, n, large)


def _t5bias_kernel(rb_ref, bucket_ref, o_ref):
    kvh = pl.program_id(0)
    bucket = bucket_ref[...]
    key = lax.broadcasted_iota(jnp.int32, bucket.shape, 0)
    qry = lax.broadcasted_iota(jnp.int32, bucket.shape, 1)
    dist = BLOCK + qry - key
    mask = (dist >= 0) & (dist < WINDOW)
    mask_first = mask & (key >= BLOCK)
    for g in range(SWA_GROUP):
        acc = jnp.zeros(bucket.shape, F32)
        for b in range(REL_BUCKETS):
            acc = jnp.where(bucket == b, rb_ref[b, kvh * SWA_GROUP + g], acc)
        cols = slice(g * BLOCK, (g + 1) * BLOCK)
        o_ref[0, :, cols] = jnp.where(mask, acc * LOG2E, NEG)
        o_ref[1, :, cols] = jnp.where(mask_first, acc * LOG2E, NEG)


def _t5bias(rel_bias):
    a = jnp.arange(BLOCK)
    bidx = jnp.arange(2 * BLOCK)
    bucket = _t5_bucket(BLOCK + a[None, :] - bidx[:, None]).astype(jnp.int32)
    return pl.pallas_call(
        _t5bias_kernel,
        grid=(SWA_KV_HEADS,),
        in_specs=[
            pl.BlockSpec(memory_space=pltpu.SMEM),
            pl.BlockSpec((2 * BLOCK, BLOCK), lambda h: (0, 0)),
        ],
        out_specs=pl.BlockSpec((2, None, 2 * BLOCK, SWA_GROUP * BLOCK), lambda h: (0, h, 0, 0)),
        out_shape=jax.ShapeDtypeStruct((2, SWA_KV_HEADS, 2 * BLOCK, SWA_GROUP * BLOCK), F32),
        compiler_params=_cparams(("arbitrary",)),
        name="t5bias",
    )(rel_bias, bucket)


def _roll_half(x):
    return pltpu.bitcast(pltpu.roll(pltpu.bitcast(x, jnp.uint32), LANES // 2, 1), x.dtype)


def _keep_lanes(x, word_mask):
    return pltpu.bitcast(pltpu.bitcast(x, jnp.uint32) & word_mask, x.dtype)


def _swa_kernel(sink_ref, q_ref, kp_ref, kc_ref, vp_ref, vc_ref, bias_ref, o_ref, s_ref, *, nqb):
    seq_start = jnp.where(pl.program_id(1) == 0, 1, 0)
    k_all = jnp.concatenate([kp_ref[...], kc_ref[...]], axis=0)
    v_all = jnp.concatenate([vp_ref[...], vc_ref[...]], axis=0)
    vt_all = v_all.astype(F32).T.astype(BF16)
    low = lax.broadcasted_iota(jnp.int32, (BLOCK, LANES), 1) < SWA_HEAD_DIM
    low2 = lax.broadcasted_iota(jnp.int32, (2 * BLOCK, LANES), 1) < SWA_HEAD_DIM
    ones = jnp.ones((BF16_ROWS, 2 * BLOCK), BF16)
    lane = lax.broadcasted_iota(jnp.int32, (BLOCK // 2, LANES), 1)
    keep_low = jnp.where(lane < SWA_HEAD_DIM, jnp.uint32(0xFFFFFFFF), jnp.uint32(0))
    keep_high = ~keep_low
    units = [(qb, kvh) for qb in range(nqb) for kvh in range(SWA_KV_HEADS)]

    def put_scores(u):
        qb, kvh = units[u]
        pair = slice(LANES * (kvh // 2), LANES * (kvh // 2 + 1))
        kb = k_all[qb * BLOCK:(qb + 2) * BLOCK, pair]
        kr = _roll_half(kb)
        kd = jnp.where(low2, kr, kb) if kvh % 2 else jnp.where(low2, kb, kr)
        qs = []
        for g in range(SWA_GROUP):
            hq = kvh * SWA_GROUP + g
            blk = q_ref[qb * BLOCK:(qb + 1) * BLOCK, LANES * (hq // 2):LANES * (hq // 2 + 1)]
            qs.append(_keep_lanes(blk, keep_low if hq % 2 == 0 else keep_high))
        s_ref[u % 3] = lax.dot_general(kd, jnp.concatenate(qs, axis=0), (((1,), (1,)), ((), ())),
                                       preferred_element_type=F32)

    put_scores(0)
    outs = []
    for u, (qb, kvh) in enumerate(units):
        if u + 1 < len(units):
            put_scores(u + 1)
        first = seq_start if qb == 0 else 0
        vth = vt_all[SWA_HEAD_DIM * kvh:SWA_HEAD_DIM * (kvh + 1), qb * BLOCK:(qb + 2) * BLOCK]
        vtx = jnp.concatenate([vth, vth, ones], axis=0)
        sink = jnp.concatenate(
            [jnp.full((1, BLOCK), sink_ref[kvh * SWA_GROUP + g] * LOG2E, F32)
             for g in range(SWA_GROUP)], axis=1)
        s = s_ref[u % 3] + bias_ref[first, kvh]
        m = jnp.maximum(jnp.max(s, axis=0, keepdims=True), sink)
        e = jnp.exp2(s - m).astype(BF16)
        ox = jnp.dot(vtx, e, preferred_element_type=F32)
        denom = ox[2 * SWA_HEAD_DIM:2 * SWA_HEAD_DIM + 1, :] + jnp.exp2(sink - m)
        o = ox[:2 * SWA_HEAD_DIM, :] / denom
        for j in range(SWA_GROUP // 2):
            a = o[:, 2 * j * BLOCK:(2 * j + 1) * BLOCK].T
            b = o[:, (2 * j + 1) * BLOCK:(2 * j + 2) * BLOCK].T
            outs.append(jnp.where(low, a, b).astype(BF16))
        if kvh == SWA_KV_HEADS - 1:
            o_ref[qb * BLOCK:(qb + 1) * BLOCK, :] = jnp.concatenate(outs, axis=1)
            outs = []


def _swa(z, bias, sinks, batch, seq):
    nqb = 4
    rows = nqb * BLOCK
    ns = seq // rows
    t = batch * seq
    qcols = SWA_HEADS * SWA_HEAD_DIM

    def cur(col):
        return pl.BlockSpec((rows, SWA_KV), lambda b, n: (b * ns + n, col))

    def prev(col):
        return pl.BlockSpec((BLOCK, SWA_KV),
                            lambda b, n: ((b * ns + n) * nqb - jnp.minimum(n, 1), col))

    return pl.pallas_call(
        functools.partial(_swa_kernel, nqb=nqb),
        grid=(batch, ns),
        in_specs=[
            pl.BlockSpec(memory_space=pltpu.SMEM),
            pl.BlockSpec((pl.Element(rows), pl.Element(qcols)),
                         lambda b, n: ((b * ns + n) * rows, Z_QS)),
            prev(Z_KS // SWA_KV), cur(Z_KS // SWA_KV),
            prev(Z_VS // SWA_KV), cur(Z_VS // SWA_KV),
            pl.BlockSpec(bias.shape, lambda b, n: (0, 0, 0, 0), pipeline_mode=pl.Buffered(1)),
        ],
        out_specs=pl.BlockSpec((rows, qcols), lambda b, n: (b * ns + n, 0)),
        out_shape=jax.ShapeDtypeStruct((t, qcols), BF16),
        scratch_shapes=[pltpu.VMEM((3, 2 * BLOCK, SWA_GROUP * BLOCK), F32)],
        compiler_params=_cparams(("parallel", "arbitrary")),
        name="swa",
    )(sinks, z, z, z, z, z, bias)


def _mix_kernel(oa_ref, ob_ref, ga_ref, gb_ref, w_ref, x_ref, g_ref, g2_ref, mod_ref, o_ref, h2_ref,
                *, tm, sub):
    for r in range(tm // sub):
        rows = slice(r * sub, (r + 1) * sub)
        ga = jax.nn.sigmoid(ga_ref[rows, :].astype(F32))
        gb = jax.nn.sigmoid(gb_ref[rows, :].astype(F32))
        u = ga * oa_ref[rows, :].astype(F32) + gb * ob_ref[rows, :].astype(F32)
        mix = jnp.dot(u.astype(BF16), w_ref[...], preferred_element_type=F32)
        x1 = x_ref[rows, :] + mod_ref[GT1:GT1 + 1, :] * _rms(mix, g_ref[...])
        o_ref[rows, :] = x1
        h2 = _rms(x1, g2_ref[...]) * (1.0 + mod_ref[SC2:SC2 + 1, :]) + mod_ref[SH2:SH2 + 1, :]
        h2_ref[rows, :] = h2.astype(BF16)


def _mix(oa, ob, z, w, x2, g, g2, mod, seq):
    t, d = x2.shape
    tm, sub = 512, 256
    tpb = seq // tm
    row = lambda i: (i, 0)
    return pl.pallas_call(
        functools.partial(_mix_kernel, tm=tm, sub=sub),
        grid=(t // tm,),
        in_specs=[
            pl.BlockSpec((tm, d), row),
            pl.BlockSpec((tm, d), row),
            pl.BlockSpec((tm, d), lambda i: (i, Z_GA // d)),
            pl.BlockSpec((tm, d), lambda i: (i, Z_GB // d)),
            pl.BlockSpec((d, d), lambda i: (0, 0), pipeline_mode=pl.Buffered(1)),
            pl.BlockSpec((tm, d), row),
            pl.BlockSpec((1, d), lambda i: (0, 0)),
            pl.BlockSpec((1, d), lambda i: (0, 0)),
            pl.BlockSpec((None, 6, d), lambda i: (i // tpb, 0, 0)),
        ],
        out_specs=[pl.BlockSpec((tm, d), row), pl.BlockSpec((tm, d), row)],
        out_shape=[jax.ShapeDtypeStruct((t, d), F32), jax.ShapeDtypeStruct((t, d), BF16)],
        compiler_params=_cparams(("parallel",)),
        name="mix",
    )(oa, ob, z, z, w, x2, g, g2, mod)


def _ffn_up_kernel(h2_ref, halo_ref, wg32_ref, wv32_ref, cwg_ref, cwv_ref, cbg_ref, cbv_ref,
                   o_ref, h_ref, wg_ref, wv_ref, *, tm, tn, sub, tpb):
    i = pl.program_id(1)

    @pl.when(i == 0)
    def _():
        wg_ref[...] = wg32_ref[...].astype(BF16)
        wv_ref[...] = wv32_ref[...].astype(BF16)

    halo = halo_ref[...]
    h_ref[0:BF16_ROWS, :] = jnp.where(i % tpb == 0, jnp.zeros_like(halo), halo)
    h_ref[BF16_ROWS:, :] = h2_ref[...]
    h = h_ref[...]

    rows = tm + BF16_ROWS
    blocks = 2 * rows // BF16_ROWS

    def pair(a, b):
        a8 = jnp.broadcast_to(a, (8, sub))
        b8 = jnp.broadcast_to(b, (8, sub))
        return pltpu.bitcast(pltpu.pack_elementwise([a8, b8], packed_dtype=BF16), BF16)

    def shift(xb):
        xi = pltpu.bitcast(xb.reshape(2 * rows, sub), jnp.int32)
        return pltpu.bitcast(pltpu.roll(xi, 1, 0), BF16).reshape(blocks, BF16_ROWS, sub)

    for c in range(tn // sub):
        cols = slice(c * sub, (c + 1) * sub)
        ug = jnp.dot(h, wg_ref[:, cols], preferred_element_type=F32)
        uv = jnp.dot(h, wv_ref[:, cols], preferred_element_type=F32)
        p = pltpu.bitcast(pltpu.pack_elementwise([ug, uv], packed_dtype=BF16), BF16)
        p = p.reshape(blocks, BF16_ROWS, sub)
        y = p * pair(cwg_ref[1:2, cols], cwv_ref[1:2, cols]) + shift(
            p * pair(cwg_ref[0:1, cols], cwv_ref[0:1, cols]))
        y = p * pair(cwg_ref[2:3, cols], cwv_ref[2:3, cols]) + shift(y)
        y = y + pair(cbg_ref[:, cols], cbv_ref[:, cols])
        yi = pltpu.bitcast(y.reshape(2 * rows, sub), jnp.int32)[BF16_ROWS:, :]
        gate = pltpu.unpack_elementwise(yi, index=0, packed_dtype=BF16, unpacked_dtype=F32)
        val = pltpu.unpack_elementwise(yi, index=1, packed_dtype=BF16, unpacked_dtype=F32)
        o_ref[:, cols] = (gate * val / (1.0 + jnp.exp2(gate * -LOG2E))).astype(o_ref.dtype)


def _ffn_up(h2, w_up, conv_w, conv_b, seq):
    t, d = h2.shape
    tm, tn, sub = 1024, 512, 256
    tpb = seq // tm
    nj = D_FF // tn
    hb = tm // BF16_ROWS
    return pl.pallas_call(
        functools.partial(_ffn_up_kernel, tm=tm, tn=tn, sub=sub, tpb=tpb),
        grid=(nj, t // tm),
        in_specs=[
            pl.BlockSpec((tm, d), lambda j, i: (i, 0)),
            pl.BlockSpec((BF16_ROWS, d), lambda j, i: (jnp.maximum(i * hb - 1, 0), 0)),
            pl.BlockSpec((d, tn), lambda j, i: (0, j)),
            pl.BlockSpec((d, tn), lambda j, i: (0, j + nj)),
            pl.BlockSpec((CONV_WIDTH, tn), lambda j, i: (0, j)),
            pl.BlockSpec((CONV_WIDTH, tn), lambda j, i: (0, j + nj)),
            pl.BlockSpec((1, tn), lambda j, i: (0, j)),
            pl.BlockSpec((1, tn), lambda j, i: (0, j + nj)),
        ],
        out_specs=pl.BlockSpec((tm, tn), lambda j, i: (i, j)),
        out_shape=jax.ShapeDtypeStruct((t, D_FF), BF16),
        scratch_shapes=[pltpu.VMEM((tm + BF16_ROWS, d), BF16),
                        pltpu.VMEM((d, tn), BF16), pltpu.VMEM((d, tn), BF16)],
        compiler_params=_cparams(("arbitrary", "arbitrary")),
        name="ffn_up",
    )(h2, h2, w_up, w_up, conv_w, conv_w, conv_b, conv_b)


def _ffn_down_kernel(a_ref, w_ref, x_ref, g_ref, mod_ref, o_ref):
    y = jnp.dot(a_ref[...], w_ref[...], preferred_element_type=F32)
    o_ref[...] = x_ref[...] + mod_ref[GT2:GT2 + 1, :] * _rms(y, g_ref[...])


def _ffn_down(a, w, x1, g, mod, seq):
    t, d = x1.shape
    tm = 256
    tpb = seq // tm
    return pl.pallas_call(
        _ffn_down_kernel,
        grid=(t // tm,),
        in_specs=[
            pl.BlockSpec((tm, D_FF), lambda i: (i, 0)),
            pl.BlockSpec((D_FF, d), lambda i: (0, 0), pipeline_mode=pl.Buffered(1)),
            pl.BlockSpec((tm, d), lambda i: (i, 0)),
            pl.BlockSpec((1, d), lambda i: (0, 0)),
            pl.BlockSpec((None, 6, d), lambda i: (i // tpb, 0, 0)),
        ],
        out_specs=pl.BlockSpec((tm, d), lambda i: (i, 0)),
        out_shape=jax.ShapeDtypeStruct((t, d), F32),
        compiler_params=_cparams(("parallel",)),
        name="ffn_down",
    )(a, w, x1, g, mod)


def _prep_w_uq(w_uq):
    w = (w_uq * (MLA_QK ** -0.5 * LOG2E)).reshape(MLA_Q_RANK, MLA_HEADS, MLA_QK)
    wn = w[..., :MLA_NOPE].reshape(MLA_Q_RANK, -1)
    wr = w[..., MLA_NOPE:].reshape(MLA_Q_RANK, -1)
    return wn.astype(BF16), wr.astype(BF16)


def _prep_w_ukv(w_ukv):
    w = w_ukv.reshape(MLA_KV_RANK, MLA_HEADS, MLA_NOPE + MLA_V)
    wk = w[..., :MLA_NOPE].reshape(MLA_KV_RANK, -1)
    wvt = w[..., MLA_NOPE:].reshape(MLA_KV_RANK, -1).T
    return wk.astype(BF16), wvt.astype(BF16)


def _rope_tables2(seq):
    pos = jnp.arange(seq, dtype=F32)
    inv = ROPE_THETA ** (-jnp.arange(0, MLA_ROPE, 2, dtype=F32) / MLA_ROPE)
    ang = pos[:, None] * inv[None, :]
    ang = jnp.concatenate([ang, ang, ang, ang], axis=-1)
    return jnp.cos(ang), jnp.sin(ang)


def kernel(x, c, w_ada, b_ada, g_pre_mix, g_post_mix, w_in, g_q_lat, w_uq, g_kv_lat, w_ukv,
           rel_bias, sinks, w_o, g_pre_ffn, g_post_ffn, w_up, conv_w, conv_b, w_down):
    batch, seq, d = x.shape
    depth = w_ada.shape[0]
    xt = x.reshape(batch * seq, d)
    c8 = jnp.pad(c, ((0, 8 - batch), (0, 0)))
    cos2, sin2 = _rope_tables2(seq)
    bias = _t5bias(rel_bias)
    for l in range(depth):
        mod = _ada(c8, w_ada[l], b_ada[l][None, :])[:batch].reshape(batch, 6, d)
        z = _inproj(_prenorm(xt, g_pre_mix[l][None, :], mod, seq), w_in[l].T)
        wn, wr = _prep_w_uq(w_uq[l])
        qn, qr = _qproj(z, g_q_lat[l][None, :], wn, wr, cos2, sin2, seq)
        wk, wvt = _prep_w_ukv(w_ukv[l])
        kn, vt, kr2 = _kvproj(z, g_kv_lat[l][None, :], wk, wvt, cos2, sin2, batch, seq)
        o_a = _mla(qn, qr, kn, kr2, vt, batch, seq)
        o_b = _swa(z, bias, sinks[l], batch, seq)
        x1, h2 = _mix(o_a, o_b, z, w_o[l].astype(BF16), xt, g_post_mix[l][None, :],
                      g_pre_ffn[l][None, :], mod, seq)
        a = _ffn_up(h2, w_up[l], conv_w[l], conv_b[l][None, :], seq)
        xt = _ffn_down(a, w_down[l].astype(BF16), x1, g_post_ffn[l][None, :], mod, seq)
    return xt.reshape(batch, seq, d)
```

```python
import functools
import math

import jax
import jax.numpy as jnp
from jax import lax
from jax.experimental import pallas as pl
from jax.experimental.pallas import tpu as pltpu

F32 = jnp.float32
BF16 = jnp.bfloat16

D_MODEL = 2048
MLA_NOPE = 128
MLA_ROPE = 64
MLA_V = 128
MLA_HEADS = D_MODEL // MLA_V
MLA_Q_RANK = 768
MLA_KV_RANK = 512
MLA_QK = MLA_NOPE + MLA_ROPE
ROPE_THETA = 10000.0
SWA_HEAD_DIM = 64
SWA_HEADS = D_MODEL // SWA_HEAD_DIM
SWA_KV_HEADS = 4
SWA_GROUP = SWA_HEADS // SWA_KV_HEADS
SWA_KV = SWA_KV_HEADS * SWA_HEAD_DIM
WINDOW = 128
BLOCK = 128
REL_BUCKETS = 32
REL_MAX_DIST = 128
D_FF = 5632
CONV_WIDTH = 3
EPS = 1e-6
NEG = -1e30
LOG2E = 1.4426950408889634

VMEM_LIMIT_BYTES = 56 * 1024 * 1024
LANES = 128
BF16_ROWS = 16
MLA_TILE = 512
MLA_VX = MLA_V + BF16_ROWS

Z_CQ = 0
Z_CKV = 768
Z_KR = 1280
Z_KRR = 1408
Z_QS = 1536
Z_KS = 3584
Z_VS = 3840
Z_GA = 4096
Z_GB = 6144
Z_COLS = 8192
W_IN_KR = MLA_Q_RANK + MLA_KV_RANK
W_IN_QS = W_IN_KR + MLA_ROPE

SH1, SC1, GT1, SH2, SC2, GT2 = range(6)


def _cparams(sem):
    return pltpu.CompilerParams(dimension_semantics=sem, vmem_limit_bytes=VMEM_LIMIT_BYTES)


def _rms(x, g):
    ms = jnp.mean(x * x, axis=-1, keepdims=True)
    return x * lax.rsqrt(ms + EPS) * g


def _ada_kernel(c_ref, w_ref, b_ref, o_ref):
    c = c_ref[...]
    ca = (c * jax.nn.sigmoid(c)).astype(BF16)
    o_ref[...] = jnp.dot(ca, w_ref[...].astype(BF16), preferred_element_type=F32) + b_ref[...]


def _ada(c8, w_ada, b_ada):
    n = w_ada.shape[1]
    tn = 1024
    return pl.pallas_call(
        _ada_kernel,
        grid=(n // tn,),
        in_specs=[
            pl.BlockSpec((8, D_MODEL), lambda j: (0, 0)),
            pl.BlockSpec((D_MODEL, tn), lambda j: (0, j)),
            pl.BlockSpec((1, tn), lambda j: (0, j)),
        ],
        out_specs=pl.BlockSpec((8, tn), lambda j: (0, j)),
        out_shape=jax.ShapeDtypeStruct((8, n), F32),
        compiler_params=_cparams(("arbitrary",)),
        name="ada",
    )(c8, w_ada, b_ada)


def _fill_normed(x_ref, g, sc, sh, h_ref, row0, nrows, chunk):
    def body(i, carry):
        r = pl.multiple_of(i * chunk, chunk)
        y = _rms(x_ref[pl.ds(r, chunk), :], g)
        h_ref[pl.ds(row0 + r, chunk), :] = (y * (1.0 + sc) + sh).astype(BF16)
        return carry

    lax.fori_loop(0, nrows // chunk, body, 0)


def _prenorm_kernel(x_ref, g_ref, mod_ref, o_ref, *, tm):
    _fill_normed(x_ref, g_ref[...], mod_ref[SC1:SC1 + 1, :], mod_ref[SH1:SH1 + 1, :],
                 o_ref, 0, tm, 128)


def _prenorm(x2, g, mod, seq):
    t, d = x2.shape
    tm = 1024
    tpb = seq // tm
    return pl.pallas_call(
        functools.partial(_prenorm_kernel, tm=tm),
        grid=(t // tm,),
        in_specs=[
            pl.BlockSpec((tm, d), lambda i: (i, 0)),
            pl.BlockSpec((1, d), lambda i: (0, 0)),
            pl.BlockSpec((None, 6, d), lambda i: (i // tpb, 0, 0)),
        ],
        out_specs=pl.BlockSpec((tm, d), lambda i: (i, 0)),
        out_shape=jax.ShapeDtypeStruct((t, d), BF16),
        compiler_params=_cparams(("parallel",)),
        name="prenorm",
    )(x2, g, mod)


INPROJ_TN = 512
KR_TILE = Z_KR // INPROJ_TN
QS_TILES = (Z_QS // INPROJ_TN, Z_KS // INPROJ_TN)


def _inproj_kernel(h_ref, *refs, nw):
    wt_refs, o_ref, wb_ref = refs[:nw], refs[nw], refs[nw + 1]
    j = pl.program_id(0)

    @pl.when(pl.program_id(1) == 0)
    def _():
        for w, wt_ref in enumerate(wt_refs):
            win = j * nw + w
            rows = slice(w * INPROJ_TN, (w + 1) * INPROJ_TN)
            is_qs = (win >= QS_TILES[0]) & (win < QS_TILES[1])
            scale = jnp.where(is_qs, SWA_HEAD_DIM ** -0.5 * LOG2E, 1.0).astype(F32)
            wb_ref[rows, :] = (wt_ref[...] * scale).astype(BF16)
            if w != KR_TILE % nw:
                continue

            @pl.when(win == KR_TILE)
            def _():
                lo = Z_KR - KR_TILE * INPROJ_TN
                kr = wt_ref[lo:lo + MLA_ROPE, :]
                half = MLA_ROPE // 2
                rot = jnp.concatenate([-kr[half:, :], kr[:half, :]], axis=0)
                wb_ref[w * INPROJ_TN + lo:(w + 1) * INPROJ_TN, :] = jnp.concatenate(
                    [kr, kr, rot, rot], axis=0).astype(BF16)

    o_ref[...] = lax.dot_general(h_ref[...], wb_ref[...], (((1,), (1,)), ((), ())),
                                 preferred_element_type=F32).astype(o_ref.dtype)


def _inproj(h, wt):
    t, d = h.shape
    tm, nw = 1024, 2
    tn = nw * INPROJ_TN

    def src_row(win):
        u = MLA_ROPE
        k = INPROJ_TN // u
        return jnp.where(win <= KR_TILE, win * k, W_IN_QS // u + (win - KR_TILE - 1) * k) * u

    def wspec(w):
        return pl.BlockSpec((pl.Element(INPROJ_TN), pl.Element(d)),
                            lambda j, i: (src_row(j * nw + w), 0))

    return pl.pallas_call(
        functools.partial(_inproj_kernel, nw=nw),
        grid=(Z_COLS // tn, t // tm),
        in_specs=[pl.BlockSpec((tm, d), lambda j, i: (i, 0))] + [wspec(w) for w in range(nw)],
        out_specs=pl.BlockSpec((tm, tn), lambda j, i: (i, j)),
        out_shape=jax.ShapeDtypeStruct((t, Z_COLS), BF16),
        scratch_shapes=[pltpu.VMEM((tn, d), BF16)],
        compiler_params=_cparams(("arbitrary", "arbitrary")),
        name="inproj",
    )(h, *([wt] * nw))


def _qproj_kernel(cq_ref, g_ref, wn_ref, wr_ref, cos_ref, sin_ref, qn_ref, qr_ref):
    cqn = _rms(cq_ref[...].astype(F32), g_ref[...]).astype(BF16)
    qn_ref[...] = jnp.dot(cqn, wn_ref[...], preferred_element_type=F32).astype(BF16)
    a = jnp.dot(cqn, wr_ref[...], preferred_element_type=F32)
    cos = cos_ref[...]
    lane = lax.broadcasted_iota(jnp.int32, cos.shape, 1)
    first = (lane % MLA_ROPE) < MLA_ROPE // 2
    sin = jnp.where(first, -sin_ref[...], sin_ref[...])
    outs = []
    for i in range(a.shape[1] // LANES):
        x = a[:, LANES * i:LANES * (i + 1)]
        rot = jnp.where(first, pltpu.roll(x, LANES - MLA_ROPE // 2, 1),
                        pltpu.roll(x, MLA_ROPE // 2, 1))
        outs.append(x * cos + rot * sin)
    qr_ref[...] = jnp.concatenate(outs, axis=1).astype(BF16)


def _qproj(z, g, wn, wr, cos2, sin2, seq):
    t = z.shape[0]
    tm = 1024
    spb = seq // tm
    hn = MLA_HEADS * MLA_NOPE
    hr = MLA_HEADS * MLA_ROPE
    return pl.pallas_call(
        _qproj_kernel,
        grid=(t // tm,),
        in_specs=[
            pl.BlockSpec((tm, MLA_Q_RANK), lambda i: (i, Z_CQ // MLA_Q_RANK)),
            pl.BlockSpec((1, MLA_Q_RANK), lambda i: (0, 0)),
            pl.BlockSpec((MLA_Q_RANK, hn), lambda i: (0, 0)),
            pl.BlockSpec((MLA_Q_RANK, hr), lambda i: (0, 0)),
            pl.BlockSpec((tm, LANES), lambda i: (i % spb, 0)),
            pl.BlockSpec((tm, LANES), lambda i: (i % spb, 0)),
        ],
        out_specs=[
            pl.BlockSpec((tm, hn), lambda i: (i, 0)),
            pl.BlockSpec((tm, hr), lambda i: (i, 0)),
        ],
        out_shape=[jax.ShapeDtypeStruct((t, hn), BF16), jax.ShapeDtypeStruct((t, hr), BF16)],
        compiler_params=_cparams(("parallel",)),
        name="qproj",
    )(z, g, wn, wr, cos2, sin2)


def _kvproj_kernel(ckv_ref, g_ref, wk_ref, wvt_ref, kr_ref, krr_ref, cos_ref, sin_ref,
                   kn_ref, vt_ref, kro_ref):
    ckvn = _rms(ckv_ref[...].astype(F32), g_ref[...]).astype(BF16)
    kn_ref[...] = jnp.dot(ckvn, wk_ref[...], preferred_element_type=F32).astype(BF16)
    vt = lax.dot_general(wvt_ref[...], ckvn, (((1,), (1,)), ((), ())), preferred_element_type=F32)
    vt_ref[:, :MLA_V, :] = vt.reshape(MLA_HEADS, MLA_V, vt.shape[1]).astype(BF16)
    vt_ref[:, MLA_V:, :] = jnp.ones((MLA_HEADS, MLA_VX - MLA_V, vt.shape[1]), BF16)
    kro_ref[...] = (kr_ref[...].astype(F32) * cos_ref[...]
                    + krr_ref[...].astype(F32) * sin_ref[...]).astype(BF16)


def _kvproj(z, g, wk, wvt, cos2, sin2, batch, seq):
    t = z.shape[0]
    tm = MLA_TILE
    spb = seq // tm
    hn = MLA_HEADS * MLA_NOPE
    hv = MLA_HEADS * MLA_V
    return pl.pallas_call(
        _kvproj_kernel,
        grid=(t // tm,),
        in_specs=[
            pl.BlockSpec((pl.Element(tm), pl.Element(MLA_KV_RANK)), lambda i: (i * tm, Z_CKV)),
            pl.BlockSpec((1, MLA_KV_RANK), lambda i: (0, 0)),
            pl.BlockSpec((MLA_KV_RANK, hn), lambda i: (0, 0)),
            pl.BlockSpec((hv, MLA_KV_RANK), lambda i: (0, 0)),
            pl.BlockSpec((tm, LANES), lambda i: (i, Z_KR // LANES)),
            pl.BlockSpec((tm, LANES), lambda i: (i, Z_KRR // LANES)),
            pl.BlockSpec((tm, LANES), lambda i: (i % spb, 0)),
            pl.BlockSpec((tm, LANES), lambda i: (i % spb, 0)),
        ],
        out_specs=[
            pl.BlockSpec((tm, hn), lambda i: (i, 0)),
            pl.BlockSpec((None, MLA_HEADS, None, MLA_VX, tm), lambda i: (i // spb, 0, i % spb, 0, 0)),
            pl.BlockSpec((tm, LANES), lambda i: (i, 0)),
        ],
        out_shape=[jax.ShapeDtypeStruct((t, hn), BF16),
                   jax.ShapeDtypeStruct((batch, MLA_HEADS, spb, MLA_VX, tm), BF16),
                   jax.ShapeDtypeStruct((t, LANES), BF16)],
        compiler_params=_cparams(("parallel",)),
        name="kvproj",
    )(z, g, wk, wvt, z, z, cos2, sin2)


def _mla_kernel(qn_ref, qr_ref, kn_ref, kr_ref, vt_ref, o_ref, q_ref, s_ref, p_ref, acc_ref,
                *, tile, nq):
    head = pl.program_id(1)
    lo = (head % 2) * MLA_ROPE
    half = tile // 2

    def scores(q, k):
        k0 = k * tile if isinstance(k, int) else pl.multiple_of(k * tile, tile)
        kc = jnp.concatenate([kn_ref[pl.ds(k0, tile), :], kr_ref[pl.ds(k0, tile), :]], axis=1)
        return lax.dot_general(kc, q, (((1,), (1,)), ((), ())), preferred_element_type=F32)

    def softmax(s, m):
        m_new = jnp.maximum(m, jnp.max(s, axis=0, keepdims=True))
        return jnp.exp2(s - m_new).astype(BF16), jnp.exp2(m - m_new), m_new

    def values(p, k):
        return jnp.dot(vt_ref[k], p, preferred_element_type=F32)

    tiles = [(qi, k) for qi in range(nq) for k in range(qi + 1)]
    nt = (((1,), (1,)), ((), ()))

    def put_scores(g):
        qi, k = tiles[g]
        rows = slice(qi * tile, (qi + 1) * tile)
        if k == 0:
            qr = qr_ref[rows, :]
            lane = lax.broadcasted_iota(jnp.int32, qr.shape, 1)
            keep = (lane >= lo) & (lane < lo + MLA_ROPE)
            q_ref[qi % 2] = jnp.concatenate(
                [qn_ref[rows, :], jnp.where(keep, qr, jnp.zeros_like(qr))], axis=1)
        q = q_ref[qi % 2]
        if k < qi:
            s_ref[g % 3] = scores(q, k)
            return
        kc = jnp.concatenate([kn_ref[rows, :], kr_ref[rows, :]], axis=1)
        s_ref[g % 3, :half, :] = lax.dot_general(kc[:half], q, nt, preferred_element_type=F32)
        s_ref[g % 3, half:, half:] = lax.dot_general(kc[half:], q[half:, :], nt,
                                                     preferred_element_type=F32)

    put_scores(0)
    m = None
    for g, (qi, k) in enumerate(tiles):
        if g + 1 < len(tiles):
            put_scores(g + 1)
        if k == 0:
            m = jnp.full((1, tile), NEG, F32)
        slot = g % 3
        if k < qi:
            p, a, m = softmax(s_ref[slot], m)
            p_ref[k % 2] = p
            pv = values(p_ref[k % 2], k)
            acc_ref[...] = pv if k == 0 else a * acc_ref[...] + pv
            continue

        key = lax.broadcasted_iota(jnp.int32, (half, half), 0)
        qry = lax.broadcasted_iota(jnp.int32, (half, half), 1)
        s_tl = jnp.where(key <= qry, s_ref[slot, :half, :half], NEG)
        s_tr = s_ref[slot, :half, half:]
        s_br = jnp.where(key <= qry, s_ref[slot, half:, half:], NEG)
        m_l = jnp.maximum(m[:, :half], jnp.max(s_tl, axis=0, keepdims=True))
        m_r = jnp.maximum(jnp.maximum(m[:, half:], jnp.max(s_tr, axis=0, keepdims=True)),
                          jnp.max(s_br, axis=0, keepdims=True))
        a = jnp.exp2(m - jnp.concatenate([m_l, m_r], axis=1))
        p_top = jnp.concatenate([jnp.exp2(s_tl - m_l), jnp.exp2(s_tr - m_r)], axis=1).astype(BF16)
        p_bot = jnp.exp2(s_br - m_r).astype(BF16)
        vt = vt_ref[qi]
        pv = jnp.dot(vt[:, :half], p_top, preferred_element_type=F32)
        pv_r = jnp.dot(vt[:, half:], p_bot, preferred_element_type=F32)
        acc = pv if qi == 0 else a * acc_ref[...] + pv
        num = jnp.concatenate([acc[:MLA_V, :half], acc[:MLA_V, half:] + pv_r[:MLA_V, :]], axis=1)
        den = jnp.concatenate([acc[MLA_V:MLA_V + 1, :half],
                               acc[MLA_V:MLA_V + 1, half:] + pv_r[MLA_V:MLA_V + 1, :]], axis=1)
        o_ref[qi * tile:(qi + 1) * tile, :] = (num / den).T.astype(o_ref.dtype)


def _mla(qn, qr, kn, kr2, vt, batch, seq):
    tile = MLA_TILE
    nq = seq // tile
    t = batch * seq
    tok = lambda b, h: (b, h)
    return pl.pallas_call(
        functools.partial(_mla_kernel, tile=tile, nq=nq),
        grid=(batch, MLA_HEADS),
        in_specs=[
            pl.BlockSpec((seq, LANES), tok),
            pl.BlockSpec((seq, LANES), lambda b, h: (b, h // 2)),
            pl.BlockSpec((seq, LANES), tok),
            pl.BlockSpec((seq, LANES), lambda b, h: (b, 0)),
            pl.BlockSpec((None, None, nq, MLA_VX, tile), lambda b, h: (b, h, 0, 0, 0)),
        ],
        out_specs=pl.BlockSpec((seq, LANES), tok),
        out_shape=jax.ShapeDtypeStruct((t, MLA_HEADS * MLA_V), BF16),
        scratch_shapes=[
            pltpu.VMEM((2, tile, 2 * LANES), BF16),
            pltpu.VMEM((3, tile, tile), F32),
            pltpu.VMEM((2, tile, tile), BF16),
            pltpu.VMEM((MLA_VX, tile), F32),
        ],
        compiler_params=_cparams(("parallel", "arbitrary")),
        name="mla",
    )(qn, qr, kn, kr2, vt)


def _t5_bucket(dist):
    max_exact = REL_BUCKETS // 2
    n = jnp.maximum(dist, 0)
    large = max_exact + (jnp.log(jnp.maximum(n, 1).astype(F32) / max_exact)
                         / math.log(REL_MAX_DIST / max_exact)
                         * (REL_BUCKETS - max_exact)).astype(jnp.int32)
    large = jnp.minimum(large, REL_BUCKETS - 1)
    return jnp.where(n < max_exact, n, large)


def _t5bias_kernel(rb_ref, bucket_ref, o_ref):
    kvh = pl.program_id(0)
    bucket = bucket_ref[...]
    key = lax.broadcasted_iota(jnp.int32, bucket.shape, 0)
    qry = lax.broadcasted_iota(jnp.int32, bucket.shape, 1)
    dist = BLOCK + qry - key
    mask = (dist >= 0) & (dist < WINDOW)
    mask_first = mask & (key >= BLOCK)
    for g in range(SWA_GROUP):
        acc = jnp.zeros(bucket.shape, F32)
        for b in range(REL_BUCKETS):
            acc = jnp.where(bucket == b, rb_ref[b, kvh * SWA_GROUP + g], acc)
        cols = slice(g * BLOCK, (g + 1) * BLOCK)
        o_ref[0, :, cols] = jnp.where(mask, acc * LOG2E, NEG)
        o_ref[1, :, cols] = jnp.where(mask_first, acc * LOG2E, NEG)


def _t5bias(rel_bias):
    a = jnp.arange(BLOCK)
    bidx = jnp.arange(2 * BLOCK)
    bucket = _t5_bucket(BLOCK + a[None, :] - bidx[:, None]).astype(jnp.int32)
    return pl.pallas_call(
        _t5bias_kernel,
        grid=(SWA_KV_HEADS,),
        in_specs=[
            pl.BlockSpec(memory_space=pltpu.SMEM),
            pl.BlockSpec((2 * BLOCK, BLOCK), lambda h: (0, 0)),
        ],
        out_specs=pl.BlockSpec((2, None, 2 * BLOCK, SWA_GROUP * BLOCK), lambda h: (0, h, 0, 0)),
        out_shape=jax.ShapeDtypeStruct((2, SWA_KV_HEADS, 2 * BLOCK, SWA_GROUP * BLOCK), F32),
        compiler_params=_cparams(("arbitrary",)),
        name="t5bias",
    )(rel_bias, bucket)


def _roll_half(x):
    return pltpu.bitcast(pltpu.roll(pltpu.bitcast(x, jnp.uint32), LANES // 2, 1), x.dtype)


def _keep_lanes(x, word_mask):
    return pltpu.bitcast(pltpu.bitcast(x, jnp.uint32) & word_mask, x.dtype)


def _swa_kernel(sink_ref, q_ref, kp_ref, kc_ref, vp_ref, vc_ref, bias_ref, o_ref, s_ref, *, nqb):
    seq_start = jnp.where(pl.program_id(1) == 0, 1, 0)
    k_all = jnp.concatenate([kp_ref[...], kc_ref[...]], axis=0)
    v_all = jnp.concatenate([vp_ref[...], vc_ref[...]], axis=0)
    vt_all = v_all.astype(F32).T.astype(BF16)
    low = lax.broadcasted_iota(jnp.int32, (BLOCK, LANES), 1) < SWA_HEAD_DIM
    low2 = lax.broadcasted_iota(jnp.int32, (2 * BLOCK, LANES), 1) < SWA_HEAD_DIM
    ones = jnp.ones((BF16_ROWS, 2 * BLOCK), BF16)
    lane = lax.broadcasted_iota(jnp.int32, (BLOCK // 2, LANES), 1)
    keep_low = jnp.where(lane < SWA_HEAD_DIM, jnp.uint32(0xFFFFFFFF), jnp.uint32(0))
    keep_high = ~keep_low
    units = [(qb, kvh) for qb in range(nqb) for kvh in range(SWA_KV_HEADS)]

    def put_scores(u):
        qb, kvh = units[u]
        pair = slice(LANES * (kvh // 2), LANES * (kvh // 2 + 1))
        kb = k_all[qb * BLOCK:(qb + 2) * BLOCK, pair]
        kr = _roll_half(kb)
        kd = jnp.where(low2, kr, kb) if kvh % 2 else jnp.where(low2, kb, kr)
        qs = []
        for g in range(SWA_GROUP):
            hq = kvh * SWA_GROUP + g
            blk = q_ref[qb * BLOCK:(qb + 1) * BLOCK, LANES * (hq // 2):LANES * (hq // 2 + 1)]
            qs.append(_keep_lanes(blk, keep_low if hq % 2 == 0 else keep_high))
        s_ref[u % 3] = lax.dot_general(kd, jnp.concatenate(qs, axis=0), (((1,), (1,)), ((), ())),
                                       preferred_element_type=F32)

    put_scores(0)
    outs = []
    for u, (qb, kvh) in enumerate(units):
        if u + 1 < len(units):
            put_scores(u + 1)
        first = seq_start if qb == 0 else 0
        vth = vt_all[SWA_HEAD_DIM * kvh:SWA_HEAD_DIM * (kvh + 1), qb * BLOCK:(qb + 2) * BLOCK]
        vtx = jnp.concatenate([vth, vth, ones], axis=0)
        sink = jnp.concatenate(
            [jnp.full((1, BLOCK), sink_ref[kvh * SWA_GROUP + g] * LOG2E, F32)
             for g in range(SWA_GROUP)], axis=1)
        s = s_ref[u % 3] + bias_ref[first, kvh]
        m = jnp.maximum(jnp.max(s, axis=0, keepdims=True), sink)
        e = jnp.exp2(s - m).astype(BF16)
        ox = jnp.dot(vtx, e, preferred_element_type=F32)
        denom = ox[2 * SWA_HEAD_DIM:2 * SWA_HEAD_DIM + 1, :] + jnp.exp2(sink - m)
        o = ox[:2 * SWA_HEAD_DIM, :] / denom
        for j in range(SWA_GROUP // 2):
            a = o[:, 2 * j * BLOCK:(2 * j + 1) * BLOCK].T
            b = o[:, (2 * j + 1) * BLOCK:(2 * j + 2) * BLOCK].T
            outs.append(jnp.where(low, a, b).astype(BF16))
        if kvh == SWA_KV_HEADS - 1:
            o_ref[qb * BLOCK:(qb + 1) * BLOCK, :] = jnp.concatenate(outs, axis=1)
            outs = []


def _swa(z, bias, sinks, batch, seq):
    nqb = 4
    rows = nqb * BLOCK
    ns = seq // rows
    t = batch * seq
    qcols = SWA_HEADS * SWA_HEAD_DIM

    def cur(col):
        return pl.BlockSpec((rows, SWA_KV), lambda b, n: (b * ns + n, col))

    def prev(col):
        return pl.BlockSpec((BLOCK, SWA_KV),
                            lambda b, n: ((b * ns + n) * nqb - jnp.minimum(n, 1), col))

    return pl.pallas_call(
        functools.partial(_swa_kernel, nqb=nqb),
        grid=(batch, ns),
        in_specs=[
            pl.BlockSpec(memory_space=pltpu.SMEM),
            pl.BlockSpec((pl.Element(rows), pl.Element(qcols)),
                         lambda b, n: ((b * ns + n) * rows, Z_QS)),
            prev(Z_KS // SWA_KV), cur(Z_KS // SWA_KV),
            prev(Z_VS // SWA_KV), cur(Z_VS // SWA_KV),
            pl.BlockSpec(bias.shape, lambda b, n: (0, 0, 0, 0), pipeline_mode=pl.Buffered(1)),
        ],
        out_specs=pl.BlockSpec((rows, qcols), lambda b, n: (b * ns + n, 0)),
        out_shape=jax.ShapeDtypeStruct((t, qcols), BF16),
        scratch_shapes=[pltpu.VMEM((3, 2 * BLOCK, SWA_GROUP * BLOCK), F32)],
        compiler_params=_cparams(("parallel", "arbitrary")),
        name="swa",
    )(sinks, z, z, z, z, z, bias)


def _mix_kernel(oa_ref, ob_ref, ga_ref, gb_ref, w_ref, x_ref, g_ref, g2_ref, mod_ref, o_ref, h2_ref,
                *, tm, sub):
    for r in range(tm // sub):
        rows = slice(r * sub, (r + 1) * sub)
        ga = jax.nn.sigmoid(ga_ref[rows, :].astype(F32))
        gb = jax.nn.sigmoid(gb_ref[rows, :].astype(F32))
        u = ga * oa_ref[rows, :].astype(F32) + gb * ob_ref[rows, :].astype(F32)
        mix = jnp.dot(u.astype(BF16), w_ref[...], preferred_element_type=F32)
        x1 = x_ref[rows, :] + mod_ref[GT1:GT1 + 1, :] * _rms(mix, g_ref[...])
        o_ref[rows, :] = x1
        h2 = _rms(x1, g2_ref[...]) * (1.0 + mod_ref[SC2:SC2 + 1, :]) + mod_ref[SH2:SH2 + 1, :]
        h2_ref[rows, :] = h2.astype(BF16)


def _mix(oa, ob, z, w, x2, g, g2, mod, seq):
    t, d = x2.shape
    tm, sub = 512, 256
    tpb = seq // tm
    row = lambda i: (i, 0)
    return pl.pallas_call(
        functools.partial(_mix_kernel, tm=tm, sub=sub),
        grid=(t // tm,),
        in_specs=[
            pl.BlockSpec((tm, d), row),
            pl.BlockSpec((tm, d), row),
            pl.BlockSpec((tm, d), lambda i: (i, Z_GA // d)),
            pl.BlockSpec((tm, d), lambda i: (i, Z_GB // d)),
            pl.BlockSpec((d, d), lambda i: (0, 0), pipeline_mode=pl.Buffered(1)),
            pl.BlockSpec((tm, d), row),
            pl.BlockSpec((1, d), lambda i: (0, 0)),
            pl.BlockSpec((1, d), lambda i: (0, 0)),
            pl.BlockSpec((None, 6, d), lambda i: (i // tpb, 0, 0)),
        ],
        out_specs=[pl.BlockSpec((tm, d), row), pl.BlockSpec((tm, d), row)],
        out_shape=[jax.ShapeDtypeStruct((t, d), F32), jax.ShapeDtypeStruct((t, d), BF16)],
        compiler_params=_cparams(("parallel",)),
        name="mix",
    )(oa, ob, z, z, w, x2, g, g2, mod)


def _ffn_up_kernel(h2_ref, wg32_ref, wv32_ref, cwg_ref, cwv_ref, cbg_ref, cbv_ref,
                   o_ref, wg_ref, wv_ref, tail_ref, *, tm, tn, sub, tpb):
    i = pl.program_id(1)

    @pl.when(i == 0)
    def _():
        wg_ref[...] = wg32_ref[...].astype(BF16)
        wv_ref[...] = wv32_ref[...].astype(BF16)
        tail_ref[...] = jnp.zeros_like(tail_ref)

    h = h2_ref[...]
    rows = tm + 8
    blocks = 2 * rows // BF16_ROWS

    def pair(a, b):
        a8 = jnp.broadcast_to(a, (8, sub))
        b8 = jnp.broadcast_to(b, (8, sub))
        return pltpu.bitcast(pltpu.pack_elementwise([a8, b8], packed_dtype=BF16), BF16)

    def shift(xb):
        xi = pltpu.bitcast(xb.reshape(2 * rows, sub), jnp.int32)
        return pltpu.bitcast(pltpu.roll(xi, 1, 0), BF16).reshape(blocks, BF16_ROWS, sub)

    for c in range(tn // sub):
        cols = slice(c * sub, (c + 1) * sub)
        ug = jnp.dot(h, wg_ref[:, cols], preferred_element_type=F32)
        uv = jnp.dot(h, wv_ref[:, cols], preferred_element_type=F32)
        new = pltpu.pack_elementwise([ug, uv], packed_dtype=BF16)
        tail = jnp.where(i % tpb == 0, jnp.zeros((8, sub), new.dtype), tail_ref[c])
        tail_ref[c] = new[tm - 8:, :]
        p = pltpu.bitcast(jnp.concatenate([tail, new], axis=0), BF16)
        p = p.reshape(blocks, BF16_ROWS, sub)
        y = p * pair(cwg_ref[1:2, cols], cwv_ref[1:2, cols]) + shift(
            p * pair(cwg_ref[0:1, cols], cwv_ref[0:1, cols]))
        y = p * pair(cwg_ref[2:3, cols], cwv_ref[2:3, cols]) + shift(y)
        y = y + pair(cbg_ref[:, cols], cbv_ref[:, cols])
        yi = pltpu.bitcast(y.reshape(2 * rows, sub), jnp.int32)[8:, :]
        gate = pltpu.unpack_elementwise(yi, index=0, packed_dtype=BF16, unpacked_dtype=F32)
        val = pltpu.unpack_elementwise(yi, index=1, packed_dtype=BF16, unpacked_dtype=F32)
        o_ref[:, cols] = (gate * val / (1.0 + jnp.exp2(gate * -LOG2E))).astype(o_ref.dtype)


def _ffn_up(h2, w_up, conv_w, conv_b, seq):
    t, d = h2.shape
    tm, tn, sub = 1024, 512, 256
    tpb = seq // tm
    nj = D_FF // tn
    return pl.pallas_call(
        functools.partial(_ffn_up_kernel, tm=tm, tn=tn, sub=sub, tpb=tpb),
        grid=(nj, t // tm),
        in_specs=[
            pl.BlockSpec((tm, d), lambda j, i: (i, 0)),
            pl.BlockSpec((d, tn), lambda j, i: (0, j)),
            pl.BlockSpec((d, tn), lambda j, i: (0, j + nj)),
            pl.BlockSpec((CONV_WIDTH, tn), lambda j, i: (0, j)),
            pl.BlockSpec((CONV_WIDTH, tn), lambda j, i: (0, j + nj)),
            pl.BlockSpec((1, tn), lambda j, i: (0, j)),
            pl.BlockSpec((1, tn), lambda j, i: (0, j + nj)),
        ],
        out_specs=pl.BlockSpec((tm, tn), lambda j, i: (i, j)),
        out_shape=jax.ShapeDtypeStruct((t, D_FF), BF16),
        scratch_shapes=[pltpu.VMEM((d, tn), BF16), pltpu.VMEM((d, tn), BF16),
                        pltpu.VMEM((tn // sub, 8, sub), jnp.int32)],
        compiler_params=_cparams(("arbitrary", "arbitrary")),
        name="ffn_up",
    )(h2, w_up, w_up, conv_w, conv_w, conv_b, conv_b)


def _ffn_down_kernel(a_ref, w_ref, x_ref, g_ref, mod_ref, o_ref):
    y = jnp.dot(a_ref[...], w_ref[...], preferred_element_type=F32)
    o_ref[...] = x_ref[...] + mod_ref[GT2:GT2 + 1, :] * _rms(y, g_ref[...])


def _ffn_down(a, w, x1, g, mod, seq):
    t, d = x1.shape
    tm = 256
    tpb = seq // tm
    return pl.pallas_call(
        _ffn_down_kernel,
        grid=(t // tm,),
        in_specs=[
            pl.BlockSpec((tm, D_FF), lambda i: (i, 0)),
            pl.BlockSpec((D_FF, d), lambda i: (0, 0), pipeline_mode=pl.Buffered(1)),
            pl.BlockSpec((tm, d), lambda i: (i, 0)),
            pl.BlockSpec((1, d), lambda i: (0, 0)),
            pl.BlockSpec((None, 6, d), lambda i: (i // tpb, 0, 0)),
        ],
        out_specs=pl.BlockSpec((tm, d), lambda i: (i, 0)),
        out_shape=jax.ShapeDtypeStruct((t, d), F32),
        compiler_params=_cparams(("parallel",)),
        name="ffn_down",
    )(a, w, x1, g, mod)


def _prep_w_uq(w_uq):
    w = (w_uq * (MLA_QK ** -0.5 * LOG2E)).reshape(MLA_Q_RANK, MLA_HEADS, MLA_QK)
    wn = w[..., :MLA_NOPE].reshape(MLA_Q_RANK, -1)
    wr = w[..., MLA_NOPE:].reshape(MLA_Q_RANK, -1)
    return wn.astype(BF16), wr.astype(BF16)


def _prep_w_ukv(w_ukv):
    w = w_ukv.reshape(MLA_KV_RANK, MLA_HEADS, MLA_NOPE + MLA_V)
    wk = w[..., :MLA_NOPE].reshape(MLA_KV_RANK, -1)
    wvt = w[..., MLA_NOPE:].reshape(MLA_KV_RANK, -1).T
    return wk.astype(BF16), wvt.astype(BF16)


def _rope_tables2(seq):
    pos = jnp.arange(seq, dtype=F32)
    inv = ROPE_THETA ** (-jnp.arange(0, MLA_ROPE, 2, dtype=F32) / MLA_ROPE)
    ang = pos[:, None] * inv[None, :]
    ang = jnp.concatenate([ang, ang, ang, ang], axis=-1)
    return jnp.cos(ang), jnp.sin(ang)


def kernel(x, c, w_ada, b_ada, g_pre_mix, g_post_mix, w_in, g_q_lat, w_uq, g_kv_lat, w_ukv,
           rel_bias, sinks, w_o, g_pre_ffn, g_post_ffn, w_up, conv_w, conv_b, w_down):
    batch, seq, d = x.shape
    depth = w_ada.shape[0]
    xt = x.reshape(batch * seq, d)
    c8 = jnp.pad(c, ((0, 8 - batch), (0, 0)))
    cos2, sin2 = _rope_tables2(seq)
    bias = _t5bias(rel_bias)
    for l in range(depth):
        mod = _ada(c8, w_ada[l], b_ada[l][None, :])[:batch].reshape(batch, 6, d)
        z = _inproj(_prenorm(xt, g_pre_mix[l][None, :], mod, seq), w_in[l].T)
        wn, wr = _prep_w_uq(w_uq[l])
        qn, qr = _qproj(z, g_q_lat[l][None, :], wn, wr, cos2, sin2, seq)
        wk, wvt = _prep_w_ukv(w_ukv[l])
        kn, vt, kr2 = _kvproj(z, g_kv_lat[l][None, :], wk, wvt, cos2, sin2, batch, seq)
        o_a = _mla(qn, qr, kn, kr2, vt, batch, seq)
        o_b = _swa(z, bias, sinks[l], batch, seq)
        x1, h2 = _mix(o_a, o_b, z, w_o[l].astype(BF16), xt, g_post_mix[l][None, :],
                      g_pre_ffn[l][None, :], mod, seq)
        a = _ffn_up(h2, w_up[l], conv_w[l], conv_b[l][None, :], seq)
        xt = _ffn_down(a, w_down[l].astype(BF16), x1, g_post_ffn[l][None, :], mod, seq)
    return xt.reshape(batch, seq, d)
```

```python
import functools
import math

import jax
import jax.numpy as jnp
from jax import lax
from jax.experimental import pallas as pl
from jax.experimental.pallas import tpu as pltpu

F32 = jnp.float32
BF16 = jnp.bfloat16

D_MODEL = 2048
MLA_NOPE = 128
MLA_ROPE = 64
MLA_V = 128
MLA_HEADS = D_MODEL // MLA_V
MLA_Q_RANK = 768
MLA_KV_RANK = 512
MLA_QK = MLA_NOPE + MLA_ROPE
ROPE_THETA = 10000.0
SWA_HEAD_DIM = 64
SWA_HEADS = D_MODEL // SWA_HEAD_DIM
SWA_KV_HEADS = 4
SWA_GROUP = SWA_HEADS // SWA_KV_HEADS
SWA_KV = SWA_KV_HEADS * SWA_HEAD_DIM
WINDOW = 128
BLOCK = 128
REL_BUCKETS = 32
REL_MAX_DIST = 128
D_FF = 5632
CONV_WIDTH = 3
EPS = 1e-6
NEG = -1e30
LOG2E = 1.4426950408889634

VMEM_LIMIT_BYTES = 56 * 1024 * 1024
LANES = 128
BF16_ROWS = 16
MLA_TILE = 512
MLA_VX = MLA_V + BF16_ROWS

Z_CQ = 0
Z_CKV = 768
Z_KR = 1280
Z_KRR = 1408
Z_QS = 1536
Z_KS = 3584
Z_VS = 3840
Z_GA = 4096
Z_GB = 6144
Z_COLS = 8192
W_IN_KR = MLA_Q_RANK + MLA_KV_RANK
W_IN_QS = W_IN_KR + MLA_ROPE

SH1, SC1, GT1, SH2, SC2, GT2 = range(6)


def _cparams(sem):
    return pltpu.CompilerParams(dimension_semantics=sem, vmem_limit_bytes=VMEM_LIMIT_BYTES)


def _rms(x, g):
    ms = jnp.mean(x * x, axis=-1, keepdims=True)
    return x * lax.rsqrt(ms + EPS) * g


def _ada_kernel(c_ref, w_ref, b_ref, o_ref):
    c = c_ref[...]
    ca = (c * jax.nn.sigmoid(c)).astype(BF16)
    o_ref[...] = jnp.dot(ca, w_ref[...].astype(BF16), preferred_element_type=F32) + b_ref[...]


def _ada(c8, w_ada, b_ada):
    n = w_ada.shape[1]
    tn = 1024
    return pl.pallas_call(
        _ada_kernel,
        grid=(n // tn,),
        in_specs=[
            pl.BlockSpec((8, D_MODEL), lambda j: (0, 0)),
            pl.BlockSpec((D_MODEL, tn), lambda j: (0, j)),
            pl.BlockSpec((1, tn), lambda j: (0, j)),
        ],
        out_specs=pl.BlockSpec((8, tn), lambda j: (0, j)),
        out_shape=jax.ShapeDtypeStruct((8, n), F32),
        compiler_params=_cparams(("arbitrary",)),
        name="ada",
    )(c8, w_ada, b_ada)


def _fill_normed(x_ref, g, sc, sh, h_ref, row0, nrows, chunk):
    gs = g * (1.0 + sc)

    def body(i, carry):
        r = pl.multiple_of(i * chunk, chunk)
        x = x_ref[pl.ds(r, chunk), :]
        y = x * lax.rsqrt(jnp.mean(x * x, axis=-1, keepdims=True) + EPS)
        h_ref[pl.ds(row0 + r, chunk), :] = (y * gs + sh).astype(BF16)
        return carry

    lax.fori_loop(0, nrows // chunk, body, 0)


def _prenorm_kernel(x_ref, g_ref, mod_ref, o_ref, *, tm):
    _fill_normed(x_ref, g_ref[...], mod_ref[SC1:SC1 + 1, :], mod_ref[SH1:SH1 + 1, :],
                 o_ref, 0, tm, 128)


def _prenorm(x2, g, mod, seq):
    t, d = x2.shape
    tm = 1024
    tpb = seq // tm
    return pl.pallas_call(
        functools.partial(_prenorm_kernel, tm=tm),
        grid=(t // tm,),
        in_specs=[
            pl.BlockSpec((tm, d), lambda i: (i, 0)),
            pl.BlockSpec((1, d), lambda i: (0, 0)),
            pl.BlockSpec((None, 6, d), lambda i: (i // tpb, 0, 0)),
        ],
        out_specs=pl.BlockSpec((tm, d), lambda i: (i, 0)),
        out_shape=jax.ShapeDtypeStruct((t, d), BF16),
        compiler_params=_cparams(("parallel",)),
        name="prenorm",
    )(x2, g, mod)


INPROJ_TN = 512
KR_TILE = Z_KR // INPROJ_TN
QS_TILES = (Z_QS // INPROJ_TN, Z_KS // INPROJ_TN)


def _inproj_kernel(h_ref, *refs, nw):
    wt_refs, o_ref, wb_ref = refs[:nw], refs[nw], refs[nw + 1]
    j = pl.program_id(0)

    @pl.when(pl.program_id(1) == 0)
    def _():
        for w, wt_ref in enumerate(wt_refs):
            win = j * nw + w
            rows = slice(w * INPROJ_TN, (w + 1) * INPROJ_TN)
            is_qs = (win >= QS_TILES[0]) & (win < QS_TILES[1])
            scale = jnp.where(is_qs, SWA_HEAD_DIM ** -0.5 * LOG2E, 1.0).astype(F32)
            wb_ref[rows, :] = (wt_ref[...] * scale).astype(BF16)
            if w != KR_TILE % nw:
                continue

            @pl.when(win == KR_TILE)
            def _():
                lo = Z_KR - KR_TILE * INPROJ_TN
                kr = wt_ref[lo:lo + MLA_ROPE, :]
                half = MLA_ROPE // 2
                rot = jnp.concatenate([-kr[half:, :], kr[:half, :]], axis=0)
                wb_ref[w * INPROJ_TN + lo:(w + 1) * INPROJ_TN, :] = jnp.concatenate(
                    [kr, kr, rot, rot], axis=0).astype(BF16)

    o_ref[...] = lax.dot_general(h_ref[...], wb_ref[...], (((1,), (1,)), ((), ())),
                                 preferred_element_type=F32).astype(o_ref.dtype)


def _inproj(h, wt):
    t, d = h.shape
    tm, nw = 1024, 2
    tn = nw * INPROJ_TN

    def src_row(win):
        u = MLA_ROPE
        k = INPROJ_TN // u
        return jnp.where(win <= KR_TILE, win * k, W_IN_QS // u + (win - KR_TILE - 1) * k) * u

    def wspec(w):
        return pl.BlockSpec((pl.Element(INPROJ_TN), pl.Element(d)),
                            lambda j, i: (src_row(j * nw + w), 0))

    return pl.pallas_call(
        functools.partial(_inproj_kernel, nw=nw),
        grid=(Z_COLS // tn, t // tm),
        in_specs=[pl.BlockSpec((tm, d), lambda j, i: (i, 0))] + [wspec(w) for w in range(nw)],
        out_specs=pl.BlockSpec((tm, tn), lambda j, i: (i, j)),
        out_shape=jax.ShapeDtypeStruct((t, Z_COLS), BF16),
        scratch_shapes=[pltpu.VMEM((tn, d), BF16)],
        compiler_params=_cparams(("arbitrary", "arbitrary")),
        name="inproj",
    )(h, *([wt] * nw))


def _qproj_kernel(cq_ref, g_ref, wn_ref, wr_ref, cos_ref, sin_ref, qn_ref, qr_ref):
    cqn = _rms(cq_ref[...].astype(F32), g_ref[...]).astype(BF16)
    nt = (((1,), (1,)), ((), ()))
    qn_ref[...] = lax.dot_general(cqn, wn_ref[...], nt, preferred_element_type=F32).astype(BF16)
    a = lax.dot_general(cqn, wr_ref[...], nt, preferred_element_type=F32)
    cos = cos_ref[...]
    lane = lax.broadcasted_iota(jnp.int32, cos.shape, 1)
    first = (lane % MLA_ROPE) < MLA_ROPE // 2
    sin = jnp.where(first, -sin_ref[...], sin_ref[...])
    outs = []
    for i in range(a.shape[1] // LANES):
        x = a[:, LANES * i:LANES * (i + 1)]
        rot = jnp.where(first, pltpu.roll(x, LANES - MLA_ROPE // 2, 1),
                        pltpu.roll(x, MLA_ROPE // 2, 1))
        outs.append(x * cos + rot * sin)
    qr_ref[...] = jnp.concatenate(outs, axis=1).astype(BF16)


def _qproj(z, g, wn, wr, cos2, sin2, seq):
    t = z.shape[0]
    tm = 1024
    spb = seq // tm
    hn = MLA_HEADS * MLA_NOPE
    hr = MLA_HEADS * MLA_ROPE
    return pl.pallas_call(
        _qproj_kernel,
        grid=(t // tm,),
        in_specs=[
            pl.BlockSpec((tm, MLA_Q_RANK), lambda i: (i, Z_CQ // MLA_Q_RANK)),
            pl.BlockSpec((1, MLA_Q_RANK), lambda i: (0, 0)),
            pl.BlockSpec((hn, MLA_Q_RANK), lambda i: (0, 0)),
            pl.BlockSpec((hr, MLA_Q_RANK), lambda i: (0, 0)),
            pl.BlockSpec((tm, LANES), lambda i: (i % spb, 0)),
            pl.BlockSpec((tm, LANES), lambda i: (i % spb, 0)),
        ],
        out_specs=[
            pl.BlockSpec((tm, hn), lambda i: (i, 0)),
            pl.BlockSpec((tm, hr), lambda i: (i, 0)),
        ],
        out_shape=[jax.ShapeDtypeStruct((t, hn), BF16), jax.ShapeDtypeStruct((t, hr), BF16)],
        compiler_params=_cparams(("parallel",)),
        name="qproj",
    )(z, g, wn, wr, cos2, sin2)


def _kvproj_kernel(ckv_ref, g_ref, wk_ref, wvt_ref, kr_ref, krr_ref, cos_ref, sin_ref,
                   kn_ref, vt_ref, kro_ref):
    ckvn = _rms(ckv_ref[...].astype(F32), g_ref[...]).astype(BF16)
    kn_ref[...] = jnp.dot(ckvn, wk_ref[...], preferred_element_type=F32).astype(BF16)
    vt = lax.dot_general(wvt_ref[...], ckvn, (((1,), (1,)), ((), ())), preferred_element_type=F32)
    vt_ref[:, :MLA_V, :] = vt.reshape(MLA_HEADS, MLA_V, vt.shape[1]).astype(BF16)
    vt_ref[:, MLA_V:, :] = jnp.ones((MLA_HEADS, MLA_VX - MLA_V, vt.shape[1]), BF16)
    kro_ref[...] = (kr_ref[...].astype(F32) * cos_ref[...]
                    + krr_ref[...].astype(F32) * sin_ref[...]).astype(BF16)


def _kvproj(z, g, wk, wvt, cos2, sin2, batch, seq):
    t = z.shape[0]
    tm = MLA_TILE
    spb = seq // tm
    hn = MLA_HEADS * MLA_NOPE
    hv = MLA_HEADS * MLA_V
    return pl.pallas_call(
        _kvproj_kernel,
        grid=(t // tm,),
        in_specs=[
            pl.BlockSpec((pl.Element(tm), pl.Element(MLA_KV_RANK)), lambda i: (i * tm, Z_CKV)),
            pl.BlockSpec((1, MLA_KV_RANK), lambda i: (0, 0)),
            pl.BlockSpec((MLA_KV_RANK, hn), lambda i: (0, 0)),
            pl.BlockSpec((hv, MLA_KV_RANK), lambda i: (0, 0)),
            pl.BlockSpec((tm, LANES), lambda i: (i, Z_KR // LANES)),
            pl.BlockSpec((tm, LANES), lambda i: (i, Z_KRR // LANES)),
            pl.BlockSpec((tm, LANES), lambda i: (i % spb, 0)),
            pl.BlockSpec((tm, LANES), lambda i: (i % spb, 0)),
        ],
        out_specs=[
            pl.BlockSpec((tm, hn), lambda i: (i, 0)),
            pl.BlockSpec((None, MLA_HEADS, None, MLA_VX, tm), lambda i: (i // spb, 0, i % spb, 0, 0)),
            pl.BlockSpec((tm, LANES), lambda i: (i, 0)),
        ],
        out_shape=[jax.ShapeDtypeStruct((t, hn), BF16),
                   jax.ShapeDtypeStruct((batch, MLA_HEADS, spb, MLA_VX, tm), BF16),
                   jax.ShapeDtypeStruct((t, LANES), BF16)],
        compiler_params=_cparams(("parallel",)),
        name="kvproj",
    )(z, g, wk, wvt, z, z, cos2, sin2)


def _mla_kernel(qn_ref, qr_ref, kn_ref, kr_ref, vt_ref, o_ref, q_ref, s_ref, p_ref, acc_ref,
                *, tile, nq):
    head = pl.program_id(1)
    lo = (head % 2) * MLA_ROPE
    half = tile // 2

    def scores(q, k):
        k0 = k * tile if isinstance(k, int) else pl.multiple_of(k * tile, tile)
        kc = jnp.concatenate([kn_ref[pl.ds(k0, tile), :], kr_ref[pl.ds(k0, tile), :]], axis=1)
        return lax.dot_general(kc, q, (((1,), (1,)), ((), ())), preferred_element_type=F32)

    def softmax(s, m):
        m_new = jnp.maximum(m, jnp.max(s, axis=0, keepdims=True))
        return jnp.exp2(s - m_new).astype(BF16), jnp.exp2(m - m_new), m_new

    def values(p, k):
        return jnp.dot(vt_ref[k], p, preferred_element_type=F32)

    tiles = [(qi, k) for qi in range(nq) for k in range(qi + 1)]
    nt = (((1,), (1,)), ((), ()))

    def put_scores(g):
        qi, k = tiles[g]
        rows = slice(qi * tile, (qi + 1) * tile)
        if k == 0:
            qr = qr_ref[rows, :]
            lane = lax.broadcasted_iota(jnp.int32, qr.shape, 1)
            keep = (lane >= lo) & (lane < lo + MLA_ROPE)
            q_ref[qi % 2] = jnp.concatenate(
                [qn_ref[rows, :], jnp.where(keep, qr, jnp.zeros_like(qr))], axis=1)
        q = q_ref[qi % 2]
        if k < qi:
            s_ref[g % 3] = scores(q, k)
            return
        kc = jnp.concatenate([kn_ref[rows, :], kr_ref[rows, :]], axis=1)
        s_ref[g % 3, :half, :] = lax.dot_general(kc[:half], q, nt, preferred_element_type=F32)
        s_ref[g % 3, half:, half:] = lax.dot_general(kc[half:], q[half:, :], nt,
                                                     preferred_element_type=F32)

    put_scores(0)
    m = None
    for g, (qi, k) in enumerate(tiles):
        if g + 1 < len(tiles):
            put_scores(g + 1)
        if k == 0:
            m = jnp.full((1, tile), NEG, F32)
        slot = g % 3
        if k < qi:
            p, a, m = softmax(s_ref[slot], m)
            p_ref[k % 2] = p
            pv = values(p_ref[k % 2], k)
            acc_ref[...] = pv if k == 0 else a * acc_ref[...] + pv
            continue

        key = lax.broadcasted_iota(jnp.int32, (half, half), 0)
        qry = lax.broadcasted_iota(jnp.int32, (half, half), 1)
        s_tl = jnp.where(key <= qry, s_ref[slot, :half, :half], NEG)
        s_tr = s_ref[slot, :half, half:]
        s_br = jnp.where(key <= qry, s_ref[slot, half:, half:], NEG)
        m_l = jnp.maximum(m[:, :half], jnp.max(s_tl, axis=0, keepdims=True))
        m_r = jnp.maximum(jnp.maximum(m[:, half:], jnp.max(s_tr, axis=0, keepdims=True)),
                          jnp.max(s_br, axis=0, keepdims=True))
        a = jnp.exp2(m - jnp.concatenate([m_l, m_r], axis=1))
        p_top = jnp.concatenate([jnp.exp2(s_tl - m_l), jnp.exp2(s_tr - m_r)], axis=1).astype(BF16)
        p_bot = jnp.exp2(s_br - m_r).astype(BF16)
        vt = vt_ref[qi]
        pv = jnp.dot(vt[:, :half], p_top, preferred_element_type=F32)
        pv_r = jnp.dot(vt[:, half:], p_bot, preferred_element_type=F32)
        acc = pv if qi == 0 else a * acc_ref[...] + pv
        num = jnp.concatenate([acc[:MLA_V, :half], acc[:MLA_V, half:] + pv_r[:MLA_V, :]], axis=1)
        den = jnp.concatenate([acc[MLA_V:MLA_V + 1, :half],
                               acc[MLA_V:MLA_V + 1, half:] + pv_r[MLA_V:MLA_V + 1, :]], axis=1)
        o_ref[qi * tile:(qi + 1) * tile, :] = (num / den).T.astype(o_ref.dtype)


def _mla(qn, qr, kn, kr2, vt, batch, seq):
    tile = MLA_TILE
    nq = seq // tile
    t = batch * seq
    tok = lambda b, h: (b, h)
    return pl.pallas_call(
        functools.partial(_mla_kernel, tile=tile, nq=nq),
        grid=(batch, MLA_HEADS),
        in_specs=[
            pl.BlockSpec((seq, LANES), tok),
            pl.BlockSpec((seq, LANES), lambda b, h: (b, h // 2)),
            pl.BlockSpec((seq, LANES), tok),
            pl.BlockSpec((seq, LANES), lambda b, h: (b, 0)),
            pl.BlockSpec((None, None, nq, MLA_VX, tile), lambda b, h: (b, h, 0, 0, 0)),
        ],
        out_specs=pl.BlockSpec((seq, LANES), tok),
        out_shape=jax.ShapeDtypeStruct((t, MLA_HEADS * MLA_V), BF16),
        scratch_shapes=[
            pltpu.VMEM((2, tile, 2 * LANES), BF16),
            pltpu.VMEM((3, tile, tile), F32),
            pltpu.VMEM((2, tile, tile), BF16),
            pltpu.VMEM((MLA_VX, tile), F32),
        ],
        compiler_params=_cparams(("parallel", "arbitrary")),
        name="mla",
    )(qn, qr, kn, kr2, vt)


def _t5_bucket(dist):
    max_exact = REL_BUCKETS // 2
    n = jnp.maximum(dist, 0)
    large = max_exact + (jnp.log(jnp.maximum(n, 1).astype(F32) / max_exact)
                         / math.log(REL_MAX_DIST / max_exact)
                         * (REL_BUCKETS - max_exact)).astype(jnp.int32)
    large = jnp.minimum(large, REL_BUCKETS - 1)
    return jnp.where(n < max_exact, n, large)


def _t5bias_kernel(rb_ref, bucket_ref, o_ref):
    kvh = pl.program_id(0)
    bucket = bucket_ref[...]
    key = lax.broadcasted_iota(jnp.int32, bucket.shape, 0)
    qry = lax.broadcasted_iota(jnp.int32, bucket.shape, 1)
    dist = BLOCK + qry - key
    mask = (dist >= 0) & (dist < WINDOW)
    mask_first = mask & (key >= BLOCK)
    for g in range(SWA_GROUP):
        acc = jnp.zeros(bucket.shape, F32)
        for b in range(REL_BUCKETS):
            acc = jnp.where(bucket == b, rb_ref[b, kvh * SWA_GROUP + g], acc)
        cols = slice(g * BLOCK, (g + 1) * BLOCK)
        o_ref[0, :, cols] = jnp.where(mask, acc * LOG2E, NEG)
        o_ref[1, :, cols] = jnp.where(mask_first, acc * LOG2E, NEG)


def _t5bias(rel_bias):
    a = jnp.arange(BLOCK)
    bidx = jnp.arange(2 * BLOCK)
    bucket = _t5_bucket(BLOCK + a[None, :] - bidx[:, None]).astype(jnp.int32)
    return pl.pallas_call(
        _t5bias_kernel,
        grid=(SWA_KV_HEADS,),
        in_specs=[
            pl.BlockSpec(memory_space=pltpu.SMEM),
            pl.BlockSpec((2 * BLOCK, BLOCK), lambda h: (0, 0)),
        ],
        out_specs=pl.BlockSpec((2, None, 2 * BLOCK, SWA_GROUP * BLOCK), lambda h: (0, h, 0, 0)),
        out_shape=jax.ShapeDtypeStruct((2, SWA_KV_HEADS, 2 * BLOCK, SWA_GROUP * BLOCK), F32),
        compiler_params=_cparams(("arbitrary",)),
        name="t5bias",
    )(rel_bias, bucket)


def _roll_half(x):
    return pltpu.bitcast(pltpu.roll(pltpu.bitcast(x, jnp.uint32), LANES // 2, 1), x.dtype)


def _keep_lanes(x, word_mask):
    return pltpu.bitcast(pltpu.bitcast(x, jnp.uint32) & word_mask, x.dtype)


def _swa_kernel(sink_ref, q_ref, kp_ref, kc_ref, vp_ref, vc_ref, bias_ref, o_ref, s_ref, *, nqb):
    seq_start = jnp.where(pl.program_id(1) == 0, 1, 0)
    k_all = jnp.concatenate([kp_ref[...], kc_ref[...]], axis=0)
    v_all = jnp.concatenate([vp_ref[...], vc_ref[...]], axis=0)
    vt_all = v_all.astype(F32).T.astype(BF16)
    low = lax.broadcasted_iota(jnp.int32, (BLOCK, LANES), 1) < SWA_HEAD_DIM
    low2 = lax.broadcasted_iota(jnp.int32, (2 * BLOCK, LANES), 1) < SWA_HEAD_DIM
    ones = jnp.ones((BF16_ROWS, 2 * BLOCK), BF16)
    lane = lax.broadcasted_iota(jnp.int32, (BLOCK // 2, LANES), 1)
    keep_low = jnp.where(lane < SWA_HEAD_DIM, jnp.uint32(0xFFFFFFFF), jnp.uint32(0))
    keep_high = ~keep_low
    units = [(qb, kvh) for qb in range(nqb) for kvh in range(SWA_KV_HEADS)]

    def put_scores(u):
        qb, kvh = units[u]
        pair = slice(LANES * (kvh // 2), LANES * (kvh // 2 + 1))
        kb = k_all[qb * BLOCK:(qb + 2) * BLOCK, pair]
        kr = _roll_half(kb)
        kd = jnp.where(low2, kr, kb) if kvh % 2 else jnp.where(low2, kb, kr)
        qs = []
        for g in range(SWA_GROUP):
            hq = kvh * SWA_GROUP + g
            blk = q_ref[qb * BLOCK:(qb + 1) * BLOCK, LANES * (hq // 2):LANES * (hq // 2 + 1)]
            qs.append(_keep_lanes(blk, keep_low if hq % 2 == 0 else keep_high))
        s_ref[u % 3] = lax.dot_general(kd, jnp.concatenate(qs, axis=0), (((1,), (1,)), ((), ())),
                                       preferred_element_type=F32)

    put_scores(0)
    outs = []
    for u, (qb, kvh) in enumerate(units):
        if u + 1 < len(units):
            put_scores(u + 1)
        first = seq_start if qb == 0 else 0
        vth = vt_all[SWA_HEAD_DIM * kvh:SWA_HEAD_DIM * (kvh + 1), qb * BLOCK:(qb + 2) * BLOCK]
        vtx = jnp.concatenate([vth, vth, ones], axis=0)
        sink = jnp.concatenate(
            [jnp.full((1, BLOCK), sink_ref[kvh * SWA_GROUP + g] * LOG2E, F32)
             for g in range(SWA_GROUP)], axis=1)
        s = s_ref[u % 3] + bias_ref[first, kvh]
        m = jnp.maximum(jnp.max(s, axis=0, keepdims=True), sink)
        e = jnp.exp2(s - m).astype(BF16)
        ox = jnp.dot(vtx, e, preferred_element_type=F32)
        denom = ox[2 * SWA_HEAD_DIM:2 * SWA_HEAD_DIM + 1, :] + jnp.exp2(sink - m)
        o = ox[:2 * SWA_HEAD_DIM, :] / denom
        for j in range(SWA_GROUP // 2):
            a = o[:, 2 * j * BLOCK:(2 * j + 1) * BLOCK].T
            b = o[:, (2 * j + 1) * BLOCK:(2 * j + 2) * BLOCK].T
            outs.append(jnp.where(low, a, b).astype(BF16))
        if kvh == SWA_KV_HEADS - 1:
            o_ref[qb * BLOCK:(qb + 1) * BLOCK, :] = jnp.concatenate(outs, axis=1)
            outs = []


def _swa(z, bias, sinks, batch, seq):
    nqb = 4
    rows = nqb * BLOCK
    ns = seq // rows
    t = batch * seq
    qcols = SWA_HEADS * SWA_HEAD_DIM

    def cur(col):
        return pl.BlockSpec((rows, SWA_KV), lambda b, n: (b * ns + n, col))

    def prev(col):
        return pl.BlockSpec((BLOCK, SWA_KV),
                            lambda b, n: ((b * ns + n) * nqb - jnp.minimum(n, 1), col))

    return pl.pallas_call(
        functools.partial(_swa_kernel, nqb=nqb),
        grid=(batch, ns),
        in_specs=[
            pl.BlockSpec(memory_space=pltpu.SMEM),
            pl.BlockSpec((pl.Element(rows), pl.Element(qcols)),
                         lambda b, n: ((b * ns + n) * rows, Z_QS)),
            prev(Z_KS // SWA_KV), cur(Z_KS // SWA_KV),
            prev(Z_VS // SWA_KV), cur(Z_VS // SWA_KV),
            pl.BlockSpec(bias.shape, lambda b, n: (0, 0, 0, 0), pipeline_mode=pl.Buffered(1)),
        ],
        out_specs=pl.BlockSpec((rows, qcols), lambda b, n: (b * ns + n, 0)),
        out_shape=jax.ShapeDtypeStruct((t, qcols), BF16),
        scratch_shapes=[pltpu.VMEM((3, 2 * BLOCK, SWA_GROUP * BLOCK), F32)],
        compiler_params=_cparams(("parallel", "arbitrary")),
        name="swa",
    )(sinks, z, z, z, z, z, bias)


def _mix_kernel(oa_ref, ob_ref, ga_ref, gb_ref, w_ref, x_ref, g_ref, g2_ref, mod_ref, o_ref, h2_ref,
                *, tm, sub):
    for r in range(tm // sub):
        rows = slice(r * sub, (r + 1) * sub)
        ga = jax.nn.sigmoid(ga_ref[rows, :].astype(F32))
        gb = jax.nn.sigmoid(gb_ref[rows, :].astype(F32))
        u = ga * oa_ref[rows, :].astype(F32) + gb * ob_ref[rows, :].astype(F32)
        mix = jnp.dot(u.astype(BF16), w_ref[...], preferred_element_type=F32)
        x1 = x_ref[rows, :] + mod_ref[GT1:GT1 + 1, :] * _rms(mix, g_ref[...])
        o_ref[rows, :] = x1
        h2 = _rms(x1, g2_ref[...]) * (1.0 + mod_ref[SC2:SC2 + 1, :]) + mod_ref[SH2:SH2 + 1, :]
        h2_ref[rows, :] = h2.astype(BF16)


def _mix(oa, ob, z, w, x2, g, g2, mod, seq):
    t, d = x2.shape
    tm, sub = 512, 256
    tpb = seq // tm
    row = lambda i: (i, 0)
    return pl.pallas_call(
        functools.partial(_mix_kernel, tm=tm, sub=sub),
        grid=(t // tm,),
        in_specs=[
            pl.BlockSpec((tm, d), row),
            pl.BlockSpec((tm, d), row),
            pl.BlockSpec((tm, d), lambda i: (i, Z_GA // d)),
            pl.BlockSpec((tm, d), lambda i: (i, Z_GB // d)),
            pl.BlockSpec((d, d), lambda i: (0, 0), pipeline_mode=pl.Buffered(1)),
            pl.BlockSpec((tm, d), row),
            pl.BlockSpec((1, d), lambda i: (0, 0)),
            pl.BlockSpec((1, d), lambda i: (0, 0)),
            pl.BlockSpec((None, 6, d), lambda i: (i // tpb, 0, 0)),
        ],
        out_specs=[pl.BlockSpec((tm, d), row), pl.BlockSpec((tm, d), row)],
        out_shape=[jax.ShapeDtypeStruct((t, d), F32), jax.ShapeDtypeStruct((t, d), BF16)],
        compiler_params=_cparams(("parallel",)),
        name="mix",
    )(oa, ob, z, z, w, x2, g, g2, mod)


def _ffn_up_kernel(h2_ref, wg32_ref, wv32_ref, cwg_ref, cwv_ref, cbg_ref, cbv_ref,
                   o_ref, wg_ref, wv_ref, tail_ref, *, tm, tn, sub, tpb):
    i = pl.program_id(1)

    @pl.when(i == 0)
    def _():
        wg_ref[...] = wg32_ref[...].astype(BF16)
        wv_ref[...] = wv32_ref[...].astype(BF16)
        tail_ref[...] = jnp.zeros_like(tail_ref)

    h = h2_ref[...]
    rows = tm + 8
    blocks = 2 * rows // BF16_ROWS

    def pair(a, b):
        a8 = jnp.broadcast_to(a, (8, sub))
        b8 = jnp.broadcast_to(b, (8, sub))
        return pltpu.bitcast(pltpu.pack_elementwise([a8, b8], packed_dtype=BF16), BF16)

    def shift(xb):
        xi = pltpu.bitcast(xb.reshape(2 * rows, sub), jnp.int32)
        return pltpu.bitcast(pltpu.roll(xi, 1, 0), BF16).reshape(blocks, BF16_ROWS, sub)

    for c in range(tn // sub):
        cols = slice(c * sub, (c + 1) * sub)
        ug = jnp.dot(h, wg_ref[:, cols], preferred_element_type=F32)
        uv = jnp.dot(h, wv_ref[:, cols], preferred_element_type=F32)
        new = pltpu.pack_elementwise([ug, uv], packed_dtype=BF16)
        tail = jnp.where(i % tpb == 0, jnp.zeros((8, sub), new.dtype), tail_ref[c])
        tail_ref[c] = new[tm - 8:, :]
        p = pltpu.bitcast(jnp.concatenate([tail, new], axis=0), BF16)
        p = p.reshape(blocks, BF16_ROWS, sub)
        y = p * pair(cwg_ref[1:2, cols], cwv_ref[1:2, cols]) + shift(
            p * pair(cwg_ref[0:1, cols], cwv_ref[0:1, cols]))
        y = p * pair(cwg_ref[2:3, cols], cwv_ref[2:3, cols]) + shift(y)
        y = y + pair(cbg_ref[:, cols], cbv_ref[:, cols])
        yi = pltpu.bitcast(y.reshape(2 * rows, sub), jnp.int32)[8:, :]
        gate = pltpu.unpack_elementwise(yi, index=0, packed_dtype=BF16, unpacked_dtype=F32)
        val = pltpu.unpack_elementwise(yi, index=1, packed_dtype=BF16, unpacked_dtype=F32)
        o_ref[:, cols] = (gate * val / (1.0 + jnp.exp2(gate * -LOG2E))).astype(o_ref.dtype)


def _ffn_up(h2, w_up, conv_w, conv_b, seq):
    t, d = h2.shape
    tm, tn, sub = 1024, 512, 256
    tpb = seq // tm
    nj = D_FF // tn
    return pl.pallas_call(
        functools.partial(_ffn_up_kernel, tm=tm, tn=tn, sub=sub, tpb=tpb),
        grid=(nj, t // tm),
        in_specs=[
            pl.BlockSpec((tm, d), lambda j, i: (i, 0)),
            pl.BlockSpec((d, tn), lambda j, i: (0, j)),
            pl.BlockSpec((d, tn), lambda j, i: (0, j + nj)),
            pl.BlockSpec((CONV_WIDTH, tn), lambda j, i: (0, j)),
            pl.BlockSpec((CONV_WIDTH, tn), lambda j, i: (0, j + nj)),
            pl.BlockSpec((1, tn), lambda j, i: (0, j)),
            pl.BlockSpec((1, tn), lambda j, i: (0, j + nj)),
        ],
        out_specs=pl.BlockSpec((tm, tn), lambda j, i: (i, j)),
        out_shape=jax.ShapeDtypeStruct((t, D_FF), BF16),
        scratch_shapes=[pltpu.VMEM((d, tn), BF16), pltpu.VMEM((d, tn), BF16),
                        pltpu.VMEM((tn // sub, 8, sub), jnp.int32)],
        compiler_params=_cparams(("arbitrary", "arbitrary")),
        name="ffn_up",
    )(h2, w_up, w_up, conv_w, conv_w, conv_b, conv_b)


def _ffn_down_kernel(a_ref, w_ref, x_ref, g_ref, mod_ref, o_ref):
    y = jnp.dot(a_ref[...], w_ref[...], preferred_element_type=F32)
    o_ref[...] = x_ref[...] + mod_ref[GT2:GT2 + 1, :] * _rms(y, g_ref[...])


def _ffn_down(a, w, x1, g, mod, seq):
    t, d = x1.shape
    tm = 256
    tpb = seq // tm
    return pl.pallas_call(
        _ffn_down_kernel,
        grid=(t // tm,),
        in_specs=[
            pl.BlockSpec((tm, D_FF), lambda i: (i, 0)),
            pl.BlockSpec((D_FF, d), lambda i: (0, 0), pipeline_mode=pl.Buffered(1)),
            pl.BlockSpec((tm, d), lambda i: (i, 0)),
            pl.BlockSpec((1, d), lambda i: (0, 0)),
            pl.BlockSpec((None, 6, d), lambda i: (i // tpb, 0, 0)),
        ],
        out_specs=pl.BlockSpec((tm, d), lambda i: (i, 0)),
        out_shape=jax.ShapeDtypeStruct((t, d), F32),
        compiler_params=_cparams(("parallel",)),
        name="ffn_down",
    )(a, w, x1, g, mod)


def _prep_w_uq(w_uq):
    w = (w_uq.T * (MLA_QK ** -0.5 * LOG2E)).reshape(MLA_HEADS, MLA_QK, MLA_Q_RANK)
    wn = w[:, :MLA_NOPE].reshape(-1, MLA_Q_RANK)
    wr = w[:, MLA_NOPE:].reshape(-1, MLA_Q_RANK)
    return wn.astype(BF16), wr.astype(BF16)


def _prep_w_ukv(w_ukv):
    w = w_ukv.reshape(MLA_KV_RANK, MLA_HEADS, MLA_NOPE + MLA_V)
    wk = w[..., :MLA_NOPE].reshape(MLA_KV_RANK, -1)
    wvt = w[..., MLA_NOPE:].reshape(MLA_KV_RANK, -1).T
    return wk.astype(BF16), wvt.astype(BF16)


def _rope_tables2(seq):
    pos = jnp.arange(seq, dtype=F32)
    inv = ROPE_THETA ** (-jnp.arange(0, MLA_ROPE, 2, dtype=F32) / MLA_ROPE)
    ang = pos[:, None] * inv[None, :]
    ang = jnp.concatenate([ang, ang, ang, ang], axis=-1)
    return jnp.cos(ang), jnp.sin(ang)


def kernel(x, c, w_ada, b_ada, g_pre_mix, g_post_mix, w_in, g_q_lat, w_uq, g_kv_lat, w_ukv,
           rel_bias, sinks, w_o, g_pre_ffn, g_post_ffn, w_up, conv_w, conv_b, w_down):
    batch, seq, d = x.shape
    depth = w_ada.shape[0]
    xt = x.reshape(batch * seq, d)
    c8 = jnp.pad(c, ((0, 8 - batch), (0, 0)))
    cos2, sin2 = _rope_tables2(seq)
    bias = _t5bias(rel_bias)
    for l in range(depth):
        mod = _ada(c8, w_ada[l], b_ada[l][None, :])[:batch].reshape(batch, 6, d)
        z = _inproj(_prenorm(xt, g_pre_mix[l][None, :], mod, seq), w_in[l].T)
        wn, wr = _prep_w_uq(w_uq[l])
        qn, qr = _qproj(z, g_q_lat[l][None, :], wn, wr, cos2, sin2, seq)
        wk, wvt = _prep_w_ukv(w_ukv[l])
        kn, vt, kr2 = _kvproj(z, g_kv_lat[l][None, :], wk, wvt, cos2, sin2, batch, seq)
        o_a = _mla(qn, qr, kn, kr2, vt, batch, seq)
        o_b = _swa(z, bias, sinks[l], batch, seq)
        x1, h2 = _mix(o_a, o_b, z, w_o[l].astype(BF16), xt, g_post_mix[l][None, :],
                      g_pre_ffn[l][None, :], mod, seq)
        a = _ffn_up(h2, w_up[l], conv_w[l], conv_b[l][None, :], seq)
        xt = _ffn_down(a, w_down[l].astype(BF16), x1, g_post_ffn[l][None, :], mod, seq)
    return xt.reshape(batch, seq, d)
```
